```python
import math
import jax
import jax.numpy as jnp
from jax import lax
import numpy as np

D_MODEL = 2048
BATCH = 1
SEQ = 8192
DEPTH = 4

PLE_DIM = 256
D_FF = 5632
N_BRANCH = 3
MIX_W = D_MODEL // 2
N_NORMS = 8
HALF = 0.5
EPS = 1e-6

RW_HEAD = 64
RW_HEADS = MIX_W // RW_HEAD
RW_DECAY_LORA = 64
RW_AAA_LORA = 64
RW_GATE_LORA = 160
RW_LNX_EPS = 64e-5
RW_COLS = 3 * MIX_W + RW_DECAY_LORA + RW_AAA_LORA + RW_GATE_LORA

DSA_HEAD_DIM = 128
DSA_HEADS = MIX_W // DSA_HEAD_DIM
DSA_KV_RANK = 256
IDX_HEADS = 16
IDX_DIM = 64
TOPK_MAX = 256
Q_BLOCK = 128
DSA_COLS = DSA_HEADS * DSA_HEAD_DIM + DSA_KV_RANK + IDX_HEADS * IDX_DIM + IDX_DIM + IDX_HEADS

ML_HEADS = 4
ML_V = MIX_W // ML_HEADS
ML_QK = ML_V // 2
ML_CONV = 4
ML_CHUNK = 128
ML_COLS = 2 * ML_HEADS * ML_QK + 2 * MIX_W + 2 * ML_HEADS

GATE_COLS = N_BRANCH * D_MODEL
N_IN = RW_COLS + DSA_COLS + ML_COLS + GATE_COLS

REL_BUCKETS = 32
REL_MAX_DIST = 128

kernel_name = 'hybrid_rwkv7_dsa_mlstm_macaron_trunk'


def _split(z, sizes):
    out, off = [], 0
    for s in sizes:
        out.append(z[..., off:off + s])
        off += s
    return out


def rms_norm(x, g):
    xf = x.astype(jnp.float32)
    y = xf * lax.rsqrt(jnp.mean(xf * xf, axis=-1, keepdims=True) + EPS)
    return (y * g.astype(jnp.float32)).astype(x.dtype)


def swiglu(x, w_gate, w_up, w_down):
    return (jax.nn.silu(x @ w_gate) * (x @ w_up)) @ w_down


def shift_right(z):
    return jnp.pad(z, ((0, 0), (1, 0), (0, 0)))[:, :-1]


def causal_dwconv(z, w):
    k_w, c = w.shape
    return lax.conv_general_dilated(z, w[:, None, :].astype(z.dtype), window_strides=(1,),
                                    padding=[(k_w - 1, 0)],
                                    dimension_numbers=('NWC', 'WIO', 'NWC'),
                                    feature_group_count=c)


def t5_bucket(dist):
    max_exact = REL_BUCKETS // 2
    n = jnp.maximum(dist, 0)
    nf = jnp.maximum(n, 1).astype(jnp.float32)
    large = max_exact + (jnp.log(nf / max_exact) / math.log(REL_MAX_DIST / max_exact)
                         * (REL_BUCKETS - max_exact)).astype(jnp.int32)
    large = jnp.minimum(large, REL_BUCKETS - 1)
    return jnp.where(n < max_exact, n, large)


def _rwkv7_step(state, inp):
    r_t, w_t, k_t, v_t, a_t, b_t = inp
    sa = jnp.einsum('bhvk,bhk->bhv', state, a_t)
    state = (state * w_t[:, :, None, :] + sa[..., None] * b_t[:, :, None, :]
             + v_t[..., None] * k_t[:, :, None, :])
    y = jnp.einsum('bhvk,bhk->bhv', state, r_t)
    return state, y


def rwkv7_mixer(z, mu, w0, w_up, a0, a_up, g_up, k_k, k_a, r_k, lnx_g, lnx_b):
    B, T, _ = z.shape
    H, N = RW_HEADS, RW_HEAD
    f32 = jnp.float32
    z = z + (shift_right(z) - z) * mu
    r, k, v, wd, ad, gd = _split(z, (MIX_W, MIX_W, MIX_W, RW_DECAY_LORA, RW_AAA_LORA, RW_GATE_LORA))
    w_log = -jax.nn.softplus(-(w0 + jnp.tanh(wd) @ w_up)) - 0.5
    decay = jnp.exp(-jnp.exp(w_log.astype(f32)))
    a = jax.nn.sigmoid((a0 + ad @ a_up).astype(f32))
    g = jax.nn.sigmoid(gd) @ g_up
    kk = (k * k_k).astype(f32).reshape(B, T, H, N)
    kk = kk / jnp.maximum(jnp.sqrt(jnp.sum(kk * kk, axis=-1, keepdims=True)), 1e-12)
    k = k.astype(f32) * (1.0 + (a - 1.0) * k_a)

    def heads(t):
        return t.astype(f32).reshape(B, T, H, N)

    rh, wh, kh, vh, ah = heads(r), heads(decay), heads(k), heads(v), heads(a)
    aa = -kk
    bb = kk * ah
    xs = tuple(jnp.moveaxis(t, 1, 0) for t in (rh, wh, kh, vh, aa, bb))
    s0 = jnp.zeros((B, H, N, N), f32)
    _, y = lax.scan(_rwkv7_step, s0, xs)
    y = jnp.moveaxis(y, 0, 1)
    mean = jnp.mean(y, axis=-1, keepdims=True)
    var = jnp.mean(jnp.square(y - mean), axis=-1, keepdims=True)
    y = ((y - mean) * lax.rsqrt(var + RW_LNX_EPS)).reshape(B, T, MIX_W) * lnx_g + lnx_b
    bonus = jnp.sum(rh * kh * r_k.reshape(H, N), axis=-1, keepdims=True) * vh
    y = y + bonus.reshape(B, T, MIX_W)
    return (y * g).astype(z.dtype)


def dsa_mixer(q, ckv, q_idx, k_idx, w_idx, ckv_g, kidx_g, w_uk, w_uv, rel_bias):
    B, T, _ = q.shape
    n_blk = T // Q_BLOCK
    topk = min(TOPK_MAX, T // 4)
    f32 = jnp.float32
    c = rms_norm(ckv, ckv_g)
    k_i = rms_norm(k_idx, kidx_g).astype(f32)
    q_lat = jnp.einsum('bthd,hdr->bthr', q.reshape(B, T, DSA_HEADS, DSA_HEAD_DIM), w_uk)
    q_i = q_idx.reshape(B, T, IDX_HEADS, IDX_DIM).astype(f32)
    w_i = w_idx.astype(f32) * (IDX_HEADS ** -0.5)
    key_pos = jnp.arange(T)

    def to_blocks(t):
        return jnp.moveaxis(t.reshape((B, n_blk, Q_BLOCK) + t.shape[2:]), 1, 0)

    def attend_block(args):
        blk, ql, qi, wi = args
        t_pos = blk * Q_BLOCK + jnp.arange(Q_BLOCK)
        rel = jnp.einsum('bqhd,bsd->bqhs', qi, k_i) * (IDX_DIM ** -0.5)
        score = jnp.einsum('bqh,bqhs->bqs', wi, jax.nn.relu(rel))
        score = jnp.where((key_pos[None, :] <= t_pos[:, None])[None], score, -jnp.inf)
        _, idx = lax.top_k(score, topk)
        c_sel = jax.vmap(lambda cb, ib: cb[ib])(c, idx)
        dist = t_pos[None, :, None] - idx
        bias = jnp.moveaxis(rel_bias[t5_bucket(dist)], -1, 1)
        logits = (jnp.einsum('bqhr,bqkr->bhqk', ql, c_sel).astype(f32)
                  * (DSA_HEAD_DIM ** -0.5) + bias)
        logits = jnp.where((dist >= 0)[:, None], logits, -jnp.inf)
        probs = jax.nn.softmax(logits, axis=-1).astype(c.dtype)
        return jnp.einsum('bhqk,bqkr->bqhr', probs, c_sel)

    o_lat = lax.map(attend_block, (jnp.arange(n_blk), to_blocks(q_lat), to_blocks(q_i), to_blocks(w_i)))
    o_lat = jnp.moveaxis(o_lat, 0, 1).reshape(B, T, DSA_HEADS, DSA_KV_RANK)
    out = jnp.einsum('bthr,hrd->bthd', o_lat, w_uv)
    return out.reshape(B, T, MIX_W)


def mlstm_mixer(q, k, v, o, ig, fg, conv_w, b_i, b_f, hn_g):
    B, T, _ = q.shape
    H, dk, dv, L = ML_HEADS, ML_QK, ML_V, ML_CHUNK
    nc = T // L
    f32 = jnp.float32
    qk = jax.nn.silu(causal_dwconv(jnp.concatenate([q, k], axis=-1), conv_w))
    q, k = qk[..., :H * dk], qk[..., H * dk:]

    def chunks(t, d):
        return t.astype(f32).reshape(B, nc, L, H, d).transpose(1, 0, 3, 2, 4)

    def gate_chunks(t):
        return t.reshape(B, nc, L, H).transpose(1, 0, 3, 2)

    qc = chunks(q, dk) * (dk ** -0.5)
    kc = chunks(k, dk)
    vc = chunks(v, dv)
    li = gate_chunks((ig + b_i).astype(f32))
    lf = gate_chunks(jax.nn.log_sigmoid((fg + b_f).astype(f32)))
    gc = jnp.cumsum(lf, axis=-1)
    tril = jnp.tril(jnp.ones((L, L), dtype=bool))

    def chunk_step(carry, inp):
        c_st, n_st, m_st = carry
        q_c, k_c, v_c, g_c, i_c = inp
        a_log = g_c + m_st[..., None]
        d_log = g_c[..., :, None] - g_c[..., None, :] + i_c[..., None, :]
        d_log = jnp.where(tril, d_log, -jnp.inf)
        m_q = jnp.maximum(a_log, jnp.max(d_log, axis=-1))
        inter = jnp.exp(a_log - m_q)
        s_w = jnp.einsum('bhjd,bhsd->bhjs', q_c, k_c) * jnp.exp(d_log - m_q[..., None])
        num = (inter[..., None] * jnp.einsum('bhjd,bhde->bhje', q_c, c_st)
               + jnp.einsum('bhjs,bhse->bhje', s_w, v_c))
        den = inter * jnp.einsum('bhjd,bhd->bhj', q_c, n_st) + jnp.sum(s_w, axis=-1)
        h = num / jnp.maximum(jnp.abs(den), jnp.exp(-m_q))[..., None]
        g_tot = g_c[..., -1]
        w_log = g_tot[..., None] - g_c + i_c
        m_new = jnp.maximum(g_tot + m_st, jnp.max(w_log, axis=-1))
        carry_decay = jnp.exp(g_tot + m_st - m_new)
        kw = k_c * jnp.exp(w_log - m_new[..., None])[..., None]
        c_new = carry_decay[..., None, None] * c_st + jnp.einsum('bhsd,bhse->bhde', kw, v_c)
        n_new = carry_decay[..., None] * n_st + jnp.sum(kw, axis=2)
        return (c_new, n_new, m_new), h

    init = (jnp.zeros((B, H, dk, dv), f32), jnp.zeros((B, H, dk), f32), jnp.zeros((B, H), f32))
    _, h = lax.scan(chunk_step, init, (qc, kc, vc, gc, li))
    h = h.transpose(1, 0, 3, 2, 4).reshape(B, T, H, dv)
    h = h * lax.rsqrt(jnp.mean(h * h, axis=-1, keepdims=True) + EPS) * hn_g.reshape(H, dv)
    return (jax.nn.sigmoid(o.astype(f32)) * h.reshape(B, T, H * dv)).astype(q.dtype)


def setup_inputs(seed: int = 0) -> dict:
    key = jax.random.key(seed)
    keys = iter(jax.random.split(key, 48))
    f32 = jnp.float32
    L, D, F = DEPTH, D_MODEL, D_FF

    def normal(shape, scale):
        return jax.random.normal(next(keys), shape, f32) * scale

    def gain(shape):
        return 1.0 + 0.02 * jax.random.normal(next(keys), shape, f32)

    return {
        'x': normal((BATCH, SEQ, D), 1.0),
        'p': normal((L, BATCH, SEQ, PLE_DIM), 1.0),
        'norm_gains': gain((L, N_NORMS, D)),
        'ffn1_gate': normal((L, D, F), D ** -0.5),
        'ffn1_up': normal((L, D, F), D ** -0.5),
        'ffn1_down': normal((L, F, D), F ** -0.5),
        'w_in': normal((L, D, N_IN), D ** -0.5),
        'rw_mu': jax.random.uniform(next(keys), (L, RW_COLS), f32),
        'rw_w0': jax.random.uniform(next(keys), (L, MIX_W), f32, minval=-6.0, maxval=-1.0),
        'rw_w_up': normal((L, RW_DECAY_LORA, MIX_W), 0.1 * RW_DECAY_LORA ** -0.5),
        'rw_a0': normal((L, MIX_W), 0.1),
        'rw_a_up': normal((L, RW_AAA_LORA, MIX_W), 0.1 * RW_AAA_LORA ** -0.5),
        'rw_g_up': normal((L, RW_GATE_LORA, MIX_W), RW_GATE_LORA ** -0.5),
        'rw_k_k': 0.85 + normal((L, MIX_W), 0.02),
        'rw_k_a': gain((L, MIX_W)),
        'rw_r_k': normal((L, MIX_W), 0.1),
        'rw_lnx_g': gain((L, MIX_W)),
        'rw_lnx_b': normal((L, MIX_W), 0.02),
        'dsa_ckv_g': gain((L, DSA_KV_RANK)),
        'dsa_kidx_g': gain((L, IDX_DIM)),
        'dsa_w_uk': normal((L, DSA_HEADS, DSA_HEAD_DIM, DSA_KV_RANK), DSA_KV_RANK ** -0.5),
        'dsa_w_uv': normal((L, DSA_HEADS, DSA_KV_RANK, DSA_HEAD_DIM), DSA_KV_RANK ** -0.5),
        'rel_bias': normal((REL_BUCKETS, DSA_HEADS), 0.5),
        'ml_conv': normal((L, ML_CONV, 2 * ML_HEADS * ML_QK), ML_CONV ** -0.5),
        'ml_b_i': normal((L, ML_HEADS), 0.1),
        'ml_b_f': 3.0 + normal((L, ML_HEADS), 0.5),
        'ml_hn_g': gain((L, MIX_W)),
        'w_branch': normal((L, N_BRANCH, MIX_W, D), MIX_W ** -0.5),
        'w_out': normal((L, D, D), D ** -0.5),
        'ffn2_gate': normal((L, D, F), D ** -0.5),
        'ffn2_up': normal((L, D, F), D ** -0.5),
        'ffn2_down': normal((L, F, D), F ** -0.5),
        'ple_proj': normal((L, PLE_DIM, D), PLE_DIM ** -0.5),
        'ple_gate': normal((L, D, D), D ** -0.5),
    }


def reference(x, p, norm_gains, ffn1_gate, ffn1_up, ffn1_down, w_in, rw_mu, rw_w0, rw_w_up,
              rw_a0, rw_a_up, rw_g_up, rw_k_k, rw_k_a, rw_r_k, rw_lnx_g, rw_lnx_b, dsa_ckv_g,
              dsa_kidx_g, dsa_w_uk, dsa_w_uv, rel_bias, ml_conv, ml_b_i, ml_b_f, ml_hn_g,
              w_branch, w_out, ffn2_gate, ffn2_up, ffn2_down, ple_proj, ple_gate):
    B, T, _ = x.shape
    h = x
    for i in range(DEPTH):
        ng = norm_gains[i]
        f = swiglu(rms_norm(h, ng[0]), ffn1_gate[i], ffn1_up[i], ffn1_down[i])
        h = h + HALF * rms_norm(f, ng[1])

        u = rms_norm(h, ng[2])
        z = u @ w_in[i]
        z_rw, z_dsa, z_ml, z_gate = _split(z, (RW_COLS, DSA_COLS, ML_COLS, GATE_COLS))
        y_rw = rwkv7_mixer(z_rw, rw_mu[i], rw_w0[i], rw_w_up[i], rw_a0[i], rw_a_up[i], rw_g_up[i],
                           rw_k_k[i], rw_k_a[i], rw_r_k[i], rw_lnx_g[i], rw_lnx_b[i])
        d_q, d_ckv, d_qi, d_ki, d_wi = _split(z_dsa, (DSA_HEADS * DSA_HEAD_DIM, DSA_KV_RANK,
                                                     IDX_HEADS * IDX_DIM, IDX_DIM, IDX_HEADS))
        y_dsa = dsa_mixer(d_q, d_ckv, d_qi, d_ki, d_wi, dsa_ckv_g[i], dsa_kidx_g[i],
                          dsa_w_uk[i], dsa_w_uv[i], rel_bias)
        m_q, m_k, m_v, m_o, m_i, m_f = _split(z_ml, (ML_HEADS * ML_QK, ML_HEADS * ML_QK, MIX_W,
                                                    MIX_W, ML_HEADS, ML_HEADS))
        y_ml = mlstm_mixer(m_q, m_k, m_v, m_o, m_i, m_f, ml_conv[i], ml_b_i[i], ml_b_f[i], ml_hn_g[i])

        ys = jnp.stack([y_rw.astype(x.dtype), y_dsa.astype(x.dtype), y_ml.astype(x.dtype)], axis=2)
        proj = jnp.einsum('btnc,ncd->btnd', ys, w_branch[i])
        gates = jax.nn.sigmoid(z_gate.reshape(B, T, N_BRANCH, D_MODEL))
        mix = jnp.sum(gates * proj, axis=2) @ w_out[i]
        h = h + rms_norm(mix, ng[3])

        f = swiglu(rms_norm(h, ng[4]), ffn2_gate[i], ffn2_up[i], ffn2_down[i])
        h = h + HALF * rms_norm(f, ng[5])

        e = p[i] @ ple_proj[i]
        pg = jax.nn.sigmoid(rms_norm(h, ng[6]) @ ple_gate[i])
        h = h + rms_norm(pg * e, ng[7])
    return h
```

```python
import functools
import math

import jax
import jax.numpy as jnp
import numpy as np
from jax import lax
from jax.experimental import pallas as pl
from jax.experimental.pallas import tpu as pltpu

F32 = jnp.float32
BF16 = jnp.bfloat16

D_MODEL = 2048
DEPTH = 4
PLE_DIM = 256
D_FF = 5632
MIX_W = 1024
EPS = 1e-6
HALF = 0.5

RW_HEAD = 64
RW_HEADS = 16
RW_DECAY_LORA = 64
RW_AAA_LORA = 64
RW_GATE_LORA = 160
RW_LNX_EPS = 64e-5
RW_COLS = 3 * MIX_W + RW_DECAY_LORA + RW_AAA_LORA + RW_GATE_LORA
RW_COLS_PAD = 3456
RW_GATE_PAD = RW_COLS_PAD - (3 * MIX_W + RW_DECAY_LORA + RW_AAA_LORA)

DSA_HEAD_DIM = 128
DSA_HEADS = 8
DSA_KV_RANK = 256
IDX_HEADS = 16
IDX_DIM = 64
TOPK_MAX = 256
DSA_COLS = DSA_HEADS * DSA_HEAD_DIM + DSA_KV_RANK + IDX_HEADS * IDX_DIM + IDX_DIM + IDX_HEADS

ML_HEADS = 4
ML_V = 256
ML_QK = 128
ML_CONV = 4
ML_CHUNK = 128
ML_COLS = 2 * ML_HEADS * ML_QK + 2 * MIX_W + 2 * ML_HEADS
ML_COLS_PAD = 3200

REL_BUCKETS = 32
REL_MAX_DIST = 128

VMEM_LIMIT_BYTES = 56 * 1024 * 1024

RW_CHUNK = 64
RW_BLOCK = 256


def _cparams(*sem):
    return pltpu.CompilerParams(dimension_semantics=sem, vmem_limit_bytes=VMEM_LIMIT_BYTES)


def _rms(x, g, eps=EPS):
    ms = jnp.mean(x * x, axis=-1, keepdims=True)
    return x * lax.rsqrt(ms + eps) * g


def _sigmoid(x):
    return 1.0 / (1.0 + jnp.exp(-x))


def _softplus(x):
    return jnp.maximum(x, 0.0) + jnp.log(1.0 + jnp.exp(-jnp.abs(x)))


def _split3(x):
    hi = x.astype(BF16)
    r1 = x - hi.astype(F32)
    mid = r1.astype(BF16)
    lo = (r1 - mid.astype(F32)).astype(BF16)
    return hi, mid, lo


def _dot(a, b):
    return jnp.dot(a.astype(BF16), b.astype(BF16), preferred_element_type=F32)


def _bdot(a, b, ca, cb):
    return lax.dot_general(a.astype(BF16), b.astype(BF16), (((ca,), (cb,)), ((0,), (0,))),
                           preferred_element_type=F32)


def _rwkv_kernel(z_ref, zprev_ref, mu_ref, w0_ref, wup_ref, a0_ref, aup_ref, gup_ref,
                 kk_ref, ka_ref, rk_ref, lng_ref, lnb_ref, tri_ref,
                 y_ref,
                 s_ref, at_ref, rt_ref, bt_ref, kt_ref, v_ref, wc_ref, y3_ref):
    i = pl.program_id(0)
    tb = z_ref.shape[0]
    nh, hd, c = RW_HEADS, RW_HEAD, RW_CHUNK
    nchunk = tb // c

    @pl.when(i == 0)
    def _():
        s_ref[...] = jnp.zeros_like(s_ref)

    z = z_ref[...]
    row = lax.broadcasted_iota(jnp.int32, (tb, 1), 0)
    prev_row = jnp.where(i == 0, 0.0, zprev_ref[7:8, :])
    shifted = jnp.where(row == 0, prev_row, pltpu.roll(z, 1, 0))
    zl = z + (shifted - z) * mu_ref[...]
    m = MIX_W
    r = zl[:, 0:m]
    k = zl[:, m:2 * m]
    v = zl[:, 2 * m:3 * m]
    wd = zl[:, 3 * m:3 * m + 64]
    ad = zl[:, 3 * m + 64:3 * m + 128]
    gd = zl[:, 3 * m + 128:]
    wpre = w0_ref[...] + _dot(jnp.tanh(wd), wup_ref[...])
    lw = -jnp.exp(-_softplus(-wpre) - 0.5)
    a = _sigmoid(a0_ref[...] + _dot(ad, aup_ref[...]))
    g = _dot(_sigmoid(gd), gup_ref[...])
    kkr = k * kk_ref[...]
    k2 = k * (1.0 + (a - 1.0) * ka_ref[...])
    rk2 = r * k2 * rk_ref[...]

    tri = tri_ref[...]
    h3, m3, l3 = _split3(lw)
    cum = (jnp.dot(tri, h3, preferred_element_type=F32) + jnp.dot(tri, m3, preferred_element_type=F32)
           + jnp.dot(tri, l3, preferred_element_type=F32))
    wc = jnp.exp(cum)
    winv = jnp.exp(-cum)
    wprev = jnp.exp(cum - lw)

    def stack(x):
        return jnp.stack([x[:, h * hd:(h + 1) * hd] for h in range(nh)], axis=0)

    kkr3 = stack(kkr)
    ss = jnp.sum(kkr3 * kkr3, axis=-1, keepdims=True)
    kk3 = kkr3 / jnp.maximum(jnp.sqrt(ss), 1e-12)
    a3 = stack(a)
    at_ref[...] = -kk3 * stack(wprev)
    winv3 = stack(winv)
    bt_ref[...] = kk3 * a3 * winv3
    kt_ref[...] = stack(k2) * winv3
    wc3 = stack(wc)
    wc_ref[...] = wc3
    r3 = stack(r)
    rt_ref[...] = r3 * wc3
    v3 = stack(v)
    v_ref[...] = v3

    ti = lax.broadcasted_iota(jnp.int32, (1, c, c), 1)
    si = lax.broadcasted_iota(jnp.int32, (1, c, c), 2)
    strict = si < ti
    incl = si <= ti
    eye = (si == ti).astype(F32)

    def chunk(ci, carry):
        sl = pl.ds(pl.multiple_of(ci * c, c), c)
        at = at_ref[:, sl, :]
        rt = rt_ref[:, sl, :]
        bt = bt_ref[:, sl, :]
        kt = kt_ref[:, sl, :]
        vv = v_ref[:, sl, :]
        wcl = wc_ref[:, pl.ds(ci * c + (c - 1), 1), :]
        ab = _bdot(at, bt, 2, 2)
        ak = _bdot(at, kt, 2, 2)
        rb = _bdot(rt, bt, 2, 2)
        rkm = _bdot(rt, kt, 2, 2)
        lmat = jnp.where(strict, ab, 0.0)
        aks = jnp.where(strict, ak, 0.0)
        rbt = jnp.where(incl, rb, 0.0)
        rkt = jnp.where(incl, rkm, 0.0)
        tinv = eye + lmat
        lp = lmat
        for _ in range(int(math.log2(c)) - 1):
            lp = _bdot(lp, lp, 2, 1)
            tinv = tinv + _bdot(lp, tinv, 2, 1)
        xv = _bdot(aks, vv, 2, 1)
        p = _bdot(tinv, at, 2, 1)
        qm = _bdot(tinv, xv, 2, 1)
        gm = rt + _bdot(rbt, p, 2, 1)
        y0 = _bdot(rbt, qm, 2, 1) + _bdot(rkt, vv, 2, 1)
        s = s_ref[...]
        u = _bdot(p, s, 2, 2) + qm
        y3_ref[:, sl, :] = _bdot(gm, s, 2, 2) + y0
        s_ref[...] = s * wcl + _bdot(u, bt * wcl, 1, 1) + _bdot(vv, kt * wcl, 1, 1)
        return carry

    lax.fori_loop(0, nchunk, chunk, 0)

    y3 = y3_ref[...]
    mean = jnp.mean(y3, axis=-1, keepdims=True)
    yc = y3 - mean
    var = jnp.mean(yc * yc, axis=-1, keepdims=True)
    yn3 = yc * lax.rsqrt(var + RW_LNX_EPS)
    bonus3 = jnp.sum(stack(rk2), axis=-1, keepdims=True) * v3
    yn = jnp.concatenate([yn3[h] for h in range(nh)], axis=-1)
    bonus = jnp.concatenate([bonus3[h] for h in range(nh)], axis=-1)
    y_ref[...] = (yn * lng_ref[...] + lnb_ref[...] + bonus) * g


def _rwkv_call(z_rw, mu, w0, w_up, a0, a_up, g_up, k_k, k_a, r_k, lnx_g, lnx_b):
    t = z_rw.shape[0]
    tb = min(RW_BLOCK, t)
    c = RW_CHUNK
    tri = jnp.asarray(np.kron(np.eye(tb // c), np.tril(np.ones((c, c)))), BF16)
    row = lambda n: pl.BlockSpec((1, n), lambda i: (0, 0))
    full = lambda a: pl.BlockSpec(a.shape, lambda i: (0,) * a.ndim)
    st = lambda: pltpu.VMEM((RW_HEADS, tb, RW_HEAD), F32)
    return pl.pallas_call(
        _rwkv_kernel,
        grid=(t // tb,),
        in_specs=[
            pl.BlockSpec((tb, RW_COLS_PAD), lambda i: (i, 0)),
            pl.BlockSpec((8, RW_COLS_PAD), lambda i: (jnp.maximum(i * (tb // 8) - 1, 0), 0)),
            row(RW_COLS_PAD), row(MIX_W), full(w_up), row(MIX_W), full(a_up), full(g_up),
            row(MIX_W), row(MIX_W), row(MIX_W), row(MIX_W), row(MIX_W), full(tri),
        ],
        out_specs=pl.BlockSpec((tb, MIX_W), lambda i: (i, 0)),
        out_shape=jax.ShapeDtypeStruct((t, MIX_W), F32),
        scratch_shapes=[pltpu.VMEM((RW_HEADS, RW_HEAD, RW_HEAD), F32),
                        st(), st(), st(), st(), st(), st(), st()],
        compiler_params=_cparams("arbitrary"),
        name="rwkv7",
    )(z_rw, z_rw, mu, w0, w_up, a0, a_up, g_up, k_k, k_a, r_k, lnx_g, lnx_b, tri)


def _proj_tok_kernel(h_ref, g_ref, w_ref, o_ref, xn_ref):
    @pl.when(pl.program_id(1) == 0)
    def _():
        xn_ref[...] = _rms(h_ref[...], g_ref[...]).astype(BF16)

    o_ref[...] = jnp.dot(xn_ref[...], w_ref[...], preferred_element_type=F32).astype(o_ref.dtype)


def _proj_tok(h, g, w, layer, tn, out_dtype=F32, tm=512):
    t, d = h.shape
    n = w.shape[2]
    tm = min(tm, t)
    return pl.pallas_call(
        _proj_tok_kernel,
        grid=(t // tm, n // tn),
        in_specs=[pl.BlockSpec((tm, d), lambda i, j: (i, 0)),
                  pl.BlockSpec((1, d), lambda i, j: (0, 0)),
                  pl.BlockSpec((None, d, tn), lambda i, j: (layer, 0, j))],
        out_specs=pl.BlockSpec((tm, tn), lambda i, j: (i, j)),
        out_shape=jax.ShapeDtypeStruct((t, n), out_dtype),
        scratch_shapes=[pltpu.VMEM((tm, d), BF16)],
        compiler_params=_cparams("parallel", "arbitrary"),
        name="proj_tok",
    )(h, g, w)


def _proj_feat_kernel(h_ref, g_ref, wt_ref, o_ref, xn_ref):
    @pl.when(pl.program_id(1) == 0)
    def _():
        xn_ref[...] = _rms(h_ref[...], g_ref[...]).astype(BF16)

    o_ref[...] = lax.dot_general(wt_ref[...], xn_ref[...], (((1,), (1,)), ((), ())),
                                 preferred_element_type=F32)


def _proj_feat(h, g, wt, layer, tn, tm=512):
    t, d = h.shape
    n = wt.shape[1]
    tm = min(tm, t)
    return pl.pallas_call(
        _proj_feat_kernel,
        grid=(t // tm, n // tn),
        in_specs=[pl.BlockSpec((tm, d), lambda i, j: (i, 0)),
                  pl.BlockSpec((1, d), lambda i, j: (0, 0)),
                  pl.BlockSpec((None, tn, d), lambda i, j: (layer, j, 0))],
        out_specs=pl.BlockSpec((tn, tm), lambda i, j: (j, i)),
        out_shape=jax.ShapeDtypeStruct((n, t), F32),
        scratch_shapes=[pltpu.VMEM((tm, d), BF16)],
        compiler_params=_cparams("parallel", "arbitrary"),
        name="proj_feat",
    )(h, g, wt)


def _ffn_kernel(h_ref, gpre_ref, gpost_ref, wg_ref, wu_ref, wd_ref, o_ref, xn_ref, acc_ref):
    j = pl.program_id(1)

    @pl.when(j == 0)
    def _():
        xn_ref[...] = _rms(h_ref[...], gpre_ref[...]).astype(BF16)
        acc_ref[...] = jnp.zeros_like(acc_ref)

    xn = xn_ref[...]
    gate = jnp.dot(xn, wg_ref[...], preferred_element_type=F32)
    up = jnp.dot(xn, wu_ref[...], preferred_element_type=F32)
    act = (gate * _sigmoid(gate) * up).astype(BF16)
    acc_ref[...] += jnp.dot(act, wd_ref[...], preferred_element_type=F32)

    @pl.when(j == pl.num_programs(1) - 1)
    def _():
        o_ref[...] = h_ref[...] + HALF * _rms(acc_ref[...], gpost_ref[...])


def _ffn(h, gpre, gpost, wg, wu, wd, layer, tm=512, tf=512):
    t, d = h.shape
    f = wg.shape[2]
    tm = min(tm, t)
    return pl.pallas_call(
        _ffn_kernel,
        grid=(t // tm, f // tf),
        in_specs=[pl.BlockSpec((tm, d), lambda i, j: (i, 0)),
                  pl.BlockSpec((1, d), lambda i, j: (0, 0)),
                  pl.BlockSpec((1, d), lambda i, j: (0, 0)),
                  pl.BlockSpec((None, d, tf), lambda i, j: (layer, 0, j)),
                  pl.BlockSpec((None, d, tf), lambda i, j: (layer, 0, j)),
                  pl.BlockSpec((None, tf, d), lambda i, j: (layer, j, 0))],
        out_specs=pl.BlockSpec((tm, d), lambda i, j: (i, 0)),
        out_shape=jax.ShapeDtypeStruct((t, d), F32),
        scratch_shapes=[pltpu.VMEM((tm, d), BF16), pltpu.VMEM((tm, d), F32)],
        compiler_params=_cparams("parallel", "arbitrary"),
        name="ffn",
    )(h, gpre, gpost, wg, wu, wd)


def _branch_kernel(yr_ref, yd_ref, ym_ref, gr_ref, gd_ref, gm_ref, wb_ref, o_ref):
    acc = _sigmoid(gr_ref[...]) * jnp.dot(yr_ref[...].astype(BF16), wb_ref[0], preferred_element_type=F32)
    acc += _sigmoid(gd_ref[...]) * jnp.dot(yd_ref[...].astype(BF16), wb_ref[1], preferred_element_type=F32)
    acc += _sigmoid(gm_ref[...]) * jnp.dot(ym_ref[...].astype(BF16), wb_ref[2], preferred_element_type=F32)
    o_ref[...] = acc.astype(o_ref.dtype)


def _branch_merge(y_rw, y_dsa, y_ml, z_gate, wb, layer, tm=512, tn=512):
    t = y_rw.shape[0]
    d = D_MODEL
    tm = min(tm, t)
    nj = d // tn
    yspec = pl.BlockSpec((tm, MIX_W), lambda i, j: (i, 0))
    gspec = lambda b: pl.BlockSpec((tm, tn), lambda i, j: (i, b * nj + j))
    return pl.pallas_call(
        _branch_kernel,
        grid=(t // tm, nj),
        in_specs=[yspec, yspec, yspec, gspec(0), gspec(1), gspec(2),
                  pl.BlockSpec((None, 3, MIX_W, tn), lambda i, j: (layer, 0, 0, j))],
        out_specs=pl.BlockSpec((tm, tn), lambda i, j: (i, j)),
        out_shape=jax.ShapeDtypeStruct((t, d), BF16),
        compiler_params=_cparams("parallel", "arbitrary"),
        name="branch_merge",
    )(y_rw, y_dsa, y_ml, z_gate, z_gate, z_gate, wb)


def _outproj_kernel(h_ref, x_ref, g_ref, w_ref, o_ref):
    mix = jnp.dot(x_ref[...], w_ref[...], preferred_element_type=F32)
    o_ref[...] = h_ref[...] + _rms(mix, g_ref[...])


def _outproj(h, x, g, w, layer, tm=512):
    t, d = h.shape
    tm = min(tm, t)
    return pl.pallas_call(
        _outproj_kernel,
        grid=(t // tm,),
        in_specs=[pl.BlockSpec((tm, d), lambda i: (i, 0)),
                  pl.BlockSpec((tm, d), lambda i: (i, 0)),
                  pl.BlockSpec((1, d), lambda i: (0, 0)),
                  pl.BlockSpec((None, d, d), lambda i: (layer, 0, 0))],
        out_specs=pl.BlockSpec((tm, d), lambda i: (i, 0)),
        out_shape=jax.ShapeDtypeStruct((t, d), F32),
        compiler_params=_cparams("parallel"),
        name="outproj",
    )(h, x, g, w)


def _ple_kernel(h_ref, p_ref, g6_ref, g7_ref, wg_ref, wp_ref, o_ref):
    h = h_ref[...]
    pg = _sigmoid(jnp.dot(_rms(h, g6_ref[...]).astype(BF16), wg_ref[...], preferred_element_type=F32))
    e = jnp.dot(p_ref[...].astype(BF16), wp_ref[...], preferred_element_type=F32)
    o_ref[...] = h + _rms(pg * e, g7_ref[...])


def _ple(h, p, g6, g7, wg, wp, layer, tm=512):
    t, d = h.shape
    tm = min(tm, t)
    return pl.pallas_call(
        _ple_kernel,
        grid=(t // tm,),
        in_specs=[pl.BlockSpec((tm, d), lambda i: (i, 0)),
                  pl.BlockSpec((None, tm, PLE_DIM), lambda i: (layer, i, 0)),
                  pl.BlockSpec((1, d), lambda i: (0, 0)),
                  pl.BlockSpec((1, d), lambda i: (0, 0)),
                  pl.BlockSpec((None, d, d), lambda i: (layer, 0, 0)),
                  pl.BlockSpec((None, PLE_DIM, d), lambda i: (layer, 0, 0))],
        out_specs=pl.BlockSpec((tm, d), lambda i: (i, 0)),
        out_shape=jax.ShapeDtypeStruct((t, d), F32),
        compiler_params=_cparams("parallel"),
        name="ple",
    )(h, p, g6, g7, wg, wp)


def _log_sigmoid(x):
    return jnp.minimum(x, 0.0) - jnp.log(1.0 + jnp.exp(-jnp.abs(x)))


def _mlstm_kernel(z_ref, gt_ref, conv_ref, bcol_ref, brow_ref, hng_ref, tri_ref, y_ref,
                  c_ref, n_ref, m_ref, prev_ref):
    i = pl.program_id(0)
    ln = z_ref.shape[0]
    nh, dk, dv = ML_HEADS, ML_QK, ML_V
    nqk = 2 * nh * dk

    @pl.when(i == 0)
    def _():
        c_ref[...] = jnp.zeros_like(c_ref)
        n_ref[...] = jnp.zeros_like(n_ref)
        m_ref[...] = jnp.zeros_like(m_ref)
        prev_ref[...] = jnp.zeros_like(prev_ref)

    x = z_ref[:, 0:nqk]
    prev = prev_ref[...]
    row8 = lax.broadcasted_iota(jnp.int32, (8, 1), 0)
    w = conv_ref[...]
    acc = x * w[ML_CONV - 1:ML_CONV, :]
    for d in range(1, ML_CONV):
        sh = pltpu.roll(x, d, 0)
        top = jnp.where(row8 < d, pltpu.roll(prev, d, 0), sh[0:8, :])
        sh = jnp.concatenate([top, sh[8:, :]], axis=0)
        acc = acc + sh * w[ML_CONV - 1 - d:ML_CONV - d, :]
    prev_ref[...] = x[ln - 8:ln, :]
    qk = acc * _sigmoid(acc)
    q = qk[:, 0:nh * dk] * (dk ** -0.5)
    k = qk[:, nh * dk:nqk]

    gcol = z_ref[:, nqk + 2 * MIX_W:nqk + 2 * MIX_W + 2 * nh] + bcol_ref[...]
    grow = gt_ref[...] + brow_ref[...]
    lcol = _log_sigmoid(gcol)
    lrow = _log_sigmoid(grow)
    tri = tri_ref[...]
    c3 = _split3(lcol)
    cum_col = sum(jnp.dot(tri, t3, preferred_element_type=F32) for t3 in c3)
    r3 = _split3(lrow)
    cum_row = sum(lax.dot_general(t3, tri, (((1,), (1,)), ((), ())), preferred_element_type=F32) for t3 in r3)

    ti = lax.broadcasted_iota(jnp.int32, (ln, ln), 0)
    si = lax.broadcasted_iota(jnp.int32, (ln, ln), 1)
    tril = si <= ti
    for h in range(nh):
        g_col = cum_col[:, nh + h:nh + h + 1]
        g_row = cum_row[nh + h:nh + h + 1, :]
        i_col = gcol[:, h:h + 1]
        i_row = grow[h:h + 1, :]
        m_st = m_ref[h, 0:1, 0:1]
        c_st = c_ref[h]
        n_st = n_ref[h, 0:1, :]
        qh = q[:, h * dk:(h + 1) * dk]
        kh = k[:, h * dk:(h + 1) * dk]
        vh = z_ref[:, nqk + h * dv:nqk + (h + 1) * dv]
        oh = z_ref[:, nqk + MIX_W + h * dv:nqk + MIX_W + (h + 1) * dv]
        a_log = g_col + m_st
        d_log = jnp.where(tril, g_col - g_row + i_row, -jnp.inf)
        m_q = jnp.maximum(a_log, jnp.max(d_log, axis=-1, keepdims=True))
        inter = jnp.exp(a_log - m_q)
        s_w = lax.dot_general(qh.astype(BF16), kh.astype(BF16), (((1,), (1,)), ((), ())),
                              preferred_element_type=F32) * jnp.exp(d_log - m_q)
        num = inter * _dot(qh, c_st) + _dot(s_w, vh)
        den = inter * jnp.sum(qh * n_st, axis=-1, keepdims=True) + jnp.sum(s_w, axis=-1, keepdims=True)
        hh = num / jnp.maximum(jnp.abs(den), jnp.exp(-m_q))
        g_tot = g_col[ln - 1:ln, :]
        w_log = g_tot - g_col + i_col
        m_new = jnp.maximum(g_tot + m_st, jnp.max(w_log, axis=0, keepdims=True))
        decay = jnp.exp(g_tot + m_st - m_new)
        kw = kh * jnp.exp(w_log - m_new)
        c_ref[h] = decay * c_st + lax.dot_general(kw.astype(BF16), vh.astype(BF16), (((0,), (0,)), ((), ())),
                                                  preferred_element_type=F32)
        n_ref[h] = jnp.broadcast_to(decay * n_st + jnp.sum(kw, axis=0, keepdims=True), (8, dk))
        m_ref[h] = jnp.broadcast_to(m_new, (8, 128))
        hn = hh * lax.rsqrt(jnp.mean(hh * hh, axis=-1, keepdims=True) + EPS) * hng_ref[:, h * dv:(h + 1) * dv]
        y_ref[:, h * dv:(h + 1) * dv] = _sigmoid(oh) * hn


def _mlstm_call(z_ml, zt, gate_row0, conv_w, b_col, b_row, hn_g):
    t = z_ml.shape[0]
    ln = ML_CHUNK
    tri = jnp.asarray(np.tril(np.ones((ln, ln))), BF16)
    full = lambda a: pl.BlockSpec(a.shape, lambda i: (0,) * a.ndim)
    return pl.pallas_call(
        _mlstm_kernel,
        grid=(t // ln,),
        in_specs=[pl.BlockSpec((ln, ML_COLS_PAD), lambda i: (i, 0)),
                  pl.BlockSpec((8, ln), lambda i: (gate_row0 // 8, i)),
                  full(conv_w), full(b_col), full(b_row), full(hn_g), full(tri)],
        out_specs=pl.BlockSpec((ln, MIX_W), lambda i: (i, 0)),
        out_shape=jax.ShapeDtypeStruct((t, MIX_W), F32),
        scratch_shapes=[pltpu.VMEM((ML_HEADS, ML_QK, ML_V), F32),
                        pltpu.VMEM((ML_HEADS, 8, ML_QK), F32),
                        pltpu.VMEM((ML_HEADS, 8, 128), F32),
                        pltpu.VMEM((8, 2 * ML_HEADS * ML_QK), F32)],
        compiler_params=_cparams("arbitrary"),
        name="mlstm",
    )(z_ml, zt, conv_w, b_col, b_row, hn_g, tri)


DSA_TQ = 256
INT_MIN = -2 ** 31
NEG = -1e30


def _t5_bucket_bounds():
    max_exact = REL_BUCKETS // 2
    d = np.arange(1, 4 * REL_MAX_DIST, dtype=np.float64)
    large = max_exact + np.floor(np.log(d / max_exact) / math.log(REL_MAX_DIST / max_exact)
                                 * (REL_BUCKETS - max_exact)).astype(np.int64)
    bucket = np.where(d < max_exact, d.astype(np.int64), np.minimum(large, REL_BUCKETS - 1))
    return [int(d[np.argmax(bucket >= b)]) for b in range(max_exact + 1, REL_BUCKETS)]


def _bias_tile_kernel(rb_ref, o_ref):
    o = pl.program_id(0)
    h = pl.program_id(1)
    tq = o_ref.shape[-1]
    s_loc = lax.broadcasted_iota(jnp.int32, (tq, tq), 0)
    t_loc = lax.broadcasted_iota(jnp.int32, (tq, tq), 1)
    dist = jnp.where(o == 2, 2 * tq, o * tq + t_loc - s_loc)
    max_exact = REL_BUCKETS // 2
    bucket = jnp.full((tq, tq), max_exact, jnp.int32)
    for bnd in _t5_bucket_bounds():
        bucket = bucket + (dist >= bnd).astype(jnp.int32)
    bucket = jnp.where(dist < max_exact, jnp.maximum(dist, 0), bucket)
    bias = jnp.zeros((tq, tq), F32)
    for b in range(REL_BUCKETS):
        bias = jnp.where(bucket == b, rb_ref[b, h], bias)
    o_ref[...] = bias


def _bias_tiles(rel_bias, tq):
    return pl.pallas_call(
        _bias_tile_kernel,
        grid=(3, DSA_HEADS),
        in_specs=[pl.BlockSpec(memory_space=pltpu.SMEM)],
        out_specs=pl.BlockSpec((None, None, tq, tq), lambda o, h: (o, h, 0, 0)),
        out_shape=jax.ShapeDtypeStruct((3, DSA_HEADS, tq, tq), F32),
        name="t5_bias_tiles",
    )(rel_bias)


def _dsa_prep_kernel(zt_ref, ckvt_ref, g_ref, gcol_ref, gk_ref, ck_ref, ki_ref, ct_ref):
    ckv = zt_ref[:, 0:DSA_KV_RANK]
    ck_ref[...] = _rms(ckv, g_ref[...]).astype(BF16)
    ki_ref[...] = _rms(zt_ref[:, DSA_KV_RANK:DSA_KV_RANK + IDX_DIM], gk_ref[...]).astype(BF16)
    ct = ckvt_ref[...]
    ms = jnp.mean(ct * ct, axis=0, keepdims=True)
    ct_ref[...] = (ct * lax.rsqrt(ms + EPS) * gcol_ref[...]).astype(BF16)


def _dsa_prep(z_dtok, zt, ckv_row0, g_row, g_col, gk_row):
    t = z_dtok.shape[0]
    tq = DSA_TQ
    full = lambda a: pl.BlockSpec(a.shape, lambda i: (0,) * a.ndim)
    return pl.pallas_call(
        _dsa_prep_kernel,
        grid=(t // tq,),
        in_specs=[pl.BlockSpec((tq, z_dtok.shape[1]), lambda i: (i, 0)),
                  pl.BlockSpec((DSA_KV_RANK, tq), lambda i: (ckv_row0 // DSA_KV_RANK, i)),
                  full(g_row), full(g_col), full(gk_row)],
        out_specs=[pl.BlockSpec((tq, DSA_KV_RANK), lambda i: (i, 0)),
                   pl.BlockSpec((tq, IDX_DIM), lambda i: (i, 0)),
                   pl.BlockSpec((None, DSA_KV_RANK, tq), lambda i: (i, 0, 0))],
        out_shape=[jax.ShapeDtypeStruct((t, DSA_KV_RANK), BF16),
                   jax.ShapeDtypeStruct((t, IDX_DIM), BF16),
                   jax.ShapeDtypeStruct((t // tq, DSA_KV_RANK, tq), BF16)],
        compiler_params=_cparams("parallel"),
        name="dsa_prep",
    )(z_dtok, zt, g_row, g_col, gk_row)


def _dsa_kernel(topk, qt_ref, qit_ref, wt_ref, ki_ref, ck_ref, ct_ref, wuk_ref, wuv_ref, bias_ref,
                o_ref, key_ref, acc_ref, m_ref, l_ref, qlat_ref, qi_ref):
    i = pl.program_id(0)
    tq = DSA_TQ
    nkb = i + 1
    nh, hd, r = DSA_HEADS, DSA_HEAD_DIM, DSA_KV_RANK

    for h in range(nh):
        ql = jnp.dot(wuk_ref[h], qt_ref[h * hd:(h + 1) * hd, :].astype(BF16), preferred_element_type=F32)
        qlat_ref[h] = (ql * (hd ** -0.5)).astype(BF16)
    qi_ref[...] = qit_ref[...].astype(BF16)
    wrow = wt_ref[...] * ((IDX_DIM ** -0.5) * (IDX_HEADS ** -0.5))
    s_loc = lax.broadcasted_iota(jnp.int32, (tq, tq), 0)
    t_loc = lax.broadcasted_iota(jnp.int32, (tq, tq), 1)

    def score_blk(kb, carry):
        kblk = ki_ref[pl.ds(pl.multiple_of(kb * tq, tq), tq), :]
        acc = jnp.zeros((tq, tq), F32)
        for h in range(IDX_HEADS):
            rel = jnp.dot(kblk, qi_ref[h * IDX_DIM:(h + 1) * IDX_DIM, :], preferred_element_type=F32)
            acc = acc + wrow[h:h + 1, :] * jnp.maximum(rel, 0.0)
        bits = pltpu.bitcast(acc, jnp.int32)
        key = jnp.where(bits < 0, bits ^ 0x7FFFFFFF, bits)
        causal = (kb * tq + s_loc) <= (i * tq + t_loc)
        key_ref[kb] = jnp.where(causal, key, INT_MIN)
        return carry

    lax.fori_loop(0, nkb, score_blk, 0)

    def count_ge(cand):
        def body(kb, cnt):
            ge = (key_ref[kb] >= cand).astype(jnp.int32)
            return cnt + jnp.sum(ge.reshape(tq // 8, 8, tq), axis=0)
        cnt = lax.fori_loop(0, nkb, body, jnp.zeros((8, tq), jnp.int32))
        return jnp.sum(cnt, axis=0, keepdims=True)

    zero = jnp.zeros((1, tq), jnp.int32)
    cur = jnp.where(count_ge(zero) >= topk, zero, INT_MIN)

    def bit_body(b, cur):
        cand = cur + jnp.left_shift(jnp.int32(1), 30 - b)
        return jnp.where(count_ge(cand) >= topk, cand, cur)

    thr = lax.fori_loop(0, 31, bit_body, cur)
    thr = jnp.maximum(thr, INT_MIN + 1)

    m_ref[...] = jnp.full_like(m_ref, NEG)
    l_ref[...] = jnp.zeros_like(l_ref)
    acc_ref[...] = jnp.zeros_like(acc_ref)

    def att_blk(kb, carry):
        addm = jnp.where(key_ref[kb] >= thr, 0.0, NEG)
        cblk = ck_ref[pl.ds(pl.multiple_of(kb * tq, tq), tq), :]
        ctb = ct_ref[kb]
        bsel = jnp.minimum(i - kb, 2)
        for h in range(nh):
            s = jnp.dot(cblk, qlat_ref[h], preferred_element_type=F32) + bias_ref[bsel, h] + addm
            m_old = m_ref[h]
            m_new = jnp.maximum(m_old, jnp.max(s, axis=0, keepdims=True))
            p = jnp.exp(s - m_new)
            corr = jnp.exp(m_old - m_new)
            l_ref[h] = l_ref[h] * corr + jnp.sum(p, axis=0, keepdims=True)
            acc_ref[h] = acc_ref[h] * corr + jnp.dot(ctb, p.astype(BF16), preferred_element_type=F32)
            m_ref[h] = m_new
        return carry

    lax.fori_loop(0, nkb, att_blk, 0)

    for h in range(nh):
        o_lat = (acc_ref[h] / l_ref[h]).astype(BF16)
        out_t = jnp.dot(wuv_ref[h], o_lat, preferred_element_type=F32)
        o_ref[:, h * hd:(h + 1) * hd] = out_t.T


def _dsa_call(zt, q_row0, qi_row0, w_row0, ki, ck, ct, wuk_t, wuv_t, bias_tiles, topk):
    t = zt.shape[1]
    tq = DSA_TQ
    nq = DSA_HEADS * DSA_HEAD_DIM
    ni = IDX_HEADS * IDX_DIM
    full = lambda a: pl.BlockSpec(a.shape, lambda i: (0,) * a.ndim)
    return pl.pallas_call(
        functools.partial(_dsa_kernel, topk),
        grid=(t // tq,),
        in_specs=[pl.BlockSpec((nq, tq), lambda i: (q_row0 // nq, i)),
                  pl.BlockSpec((ni, tq), lambda i: (qi_row0 // ni, i)),
                  pl.BlockSpec((IDX_HEADS, tq), lambda i: (w_row0 // IDX_HEADS, i)),
                  full(ki), full(ck), full(ct), full(wuk_t), full(wuv_t), full(bias_tiles)],
        out_specs=pl.BlockSpec((tq, MIX_W), lambda i: (i, 0)),
        out_shape=jax.ShapeDtypeStruct((t, MIX_W), F32),
        scratch_shapes=[pltpu.VMEM((t // tq, tq, tq), jnp.int32),
                        pltpu.VMEM((DSA_HEADS, DSA_KV_RANK, tq), F32),
                        pltpu.VMEM((DSA_HEADS, 1, tq), F32),
                        pltpu.VMEM((DSA_HEADS, 1, tq), F32),
                        pltpu.VMEM((DSA_HEADS, DSA_KV_RANK, tq), BF16),
                        pltpu.VMEM((ni, tq), BF16)],
        compiler_params=_cparams("arbitrary"),
        name="dsa",
    )(zt, zt, zt, ki, ck, ct, wuk_t, wuv_t, bias_tiles)


FEAT_Q0 = 0
FEAT_QI0 = FEAT_Q0 + DSA_HEADS * DSA_HEAD_DIM
FEAT_CKV0 = FEAT_QI0 + IDX_HEADS * IDX_DIM
FEAT_W0 = FEAT_CKV0 + DSA_KV_RANK
FEAT_MLG0 = FEAT_W0 + IDX_HEADS
FEAT_ROWS = 2560
DTOK_COLS = 384


def _pad_last(a, n):
    return jnp.pad(a, [(0, 0)] * (a.ndim - 1) + [(0, n - a.shape[-1])])


def kernel(x, p, norm_gains, ffn1_gate, ffn1_up, ffn1_down, w_in, rw_mu, rw_w0, rw_w_up, rw_a0, rw_a_up, rw_g_up, rw_k_k, rw_k_a, rw_r_k, rw_lnx_g, rw_lnx_b, dsa_ckv_g, dsa_kidx_g, dsa_w_uk, dsa_w_uv, rel_bias, ml_conv, ml_b_i, ml_b_f, ml_hn_g, w_branch, w_out, ffn2_gate, ffn2_up, ffn2_down, ple_proj, ple_gate):
    b, t, d = x.shape
    assert b == 1 and d == D_MODEL
    nl = w_in.shape[0]
    h = x.reshape(t, d)
    p = p.reshape(nl, t, PLE_DIM)
    bf = lambda a: a.astype(BF16)

    f1g, f1u, f1d = bf(ffn1_gate), bf(ffn1_up), bf(ffn1_down)
    f2g, f2u, f2d = bf(ffn2_gate), bf(ffn2_up), bf(ffn2_down)
    wb, wo, wpg, wpp = bf(w_branch), bf(w_out), bf(ple_gate), bf(ple_proj)
    o_dsa = RW_COLS
    o_ml = o_dsa + DSA_COLS
    o_gate = o_ml + ML_COLS
    w_rw = bf(_pad_last(w_in[:, :, 0:RW_COLS], RW_COLS_PAD))
    w_ml = bf(_pad_last(w_in[:, :, o_ml:o_gate], ML_COLS_PAD))
    w_gate = bf(w_in[:, :, o_gate:])
    nq = DSA_HEADS * DSA_HEAD_DIM
    ni = IDX_HEADS * IDX_DIM
    w_q = w_in[:, :, o_dsa:o_dsa + nq]
    w_ckv = w_in[:, :, o_dsa + nq:o_dsa + nq + DSA_KV_RANK]
    w_qi = w_in[:, :, o_dsa + nq + DSA_KV_RANK:o_dsa + nq + DSA_KV_RANK + ni]
    w_ki = w_in[:, :, o_dsa + nq + DSA_KV_RANK + ni:o_dsa + nq + DSA_KV_RANK + ni + IDX_DIM]
    w_wi = w_in[:, :, o_dsa + DSA_COLS - IDX_HEADS:o_dsa + DSA_COLS]
    w_mlg = w_in[:, :, o_gate - 2 * ML_HEADS:o_gate]
    w_dtok = bf(_pad_last(jnp.concatenate([w_ckv, w_ki], axis=-1), DTOK_COLS))
    w_feat = bf(jnp.swapaxes(_pad_last(jnp.concatenate([w_q, w_qi, w_ckv, w_wi, w_mlg], axis=-1), FEAT_ROWS), 1, 2))
    mu = _pad_last(rw_mu, RW_COLS_PAD)
    wup, aup = bf(rw_w_up), bf(rw_a_up)
    gup = bf(jnp.pad(rw_g_up, ((0, 0), (0, RW_GATE_PAD - RW_GATE_LORA), (0, 0))))
    wuk_t = bf(jnp.swapaxes(dsa_w_uk, 2, 3))
    wuv_t = bf(jnp.swapaxes(dsa_w_uv, 2, 3))
    ml_b = jnp.concatenate([ml_b_i, ml_b_f], axis=-1)

    bias_tiles = _bias_tiles(rel_bias, DSA_TQ)
    topk = min(TOPK_MAX, t // 4)
    row = lambda a: a.reshape(1, -1)

    for i in range(nl):
        g = lambda k: norm_gains[i, k].reshape(1, d)
        h = _ffn(h, g(0), g(1), f1g, f1u, f1d, i)

        z_rw = _proj_tok(h, g(2), w_rw, i, tn=1152)
        z_ml = _proj_tok(h, g(2), w_ml, i, tn=640)
        z_gate = _proj_tok(h, g(2), w_gate, i, tn=1024)
        z_dtok = _proj_tok(h, g(2), w_dtok, i, tn=DTOK_COLS)
        zt = _proj_feat(h, g(2), w_feat, i, tn=512)

        y_rw = _rwkv_call(z_rw, row(mu[i]), row(rw_w0[i]), wup[i], row(rw_a0[i]), aup[i], gup[i],
                          row(rw_k_k[i]), row(rw_k_a[i]), row(rw_r_k[i]), row(rw_lnx_g[i]), row(rw_lnx_b[i]))
        ck, ki, ct = _dsa_prep(z_dtok, zt, FEAT_CKV0, row(dsa_ckv_g[i]), dsa_ckv_g[i].reshape(-1, 1),
                               row(dsa_kidx_g[i]))
        y_dsa = _dsa_call(zt, FEAT_Q0, FEAT_QI0, FEAT_W0, ki, ck, ct, wuk_t[i], wuv_t[i], bias_tiles, topk)
        y_ml = _mlstm_call(z_ml, zt, FEAT_MLG0, ml_conv[i], row(ml_b[i]), ml_b[i].reshape(-1, 1), row(ml_hn_g[i]))

        mixsum = _branch_merge(y_rw, y_dsa, y_ml, z_gate, wb, i)
        h = _outproj(h, mixsum, g(3), wo, i)
        h = _ffn(h, g(4), g(5), f2g, f2u, f2d, i)
        h = _ple(h, p, g(6), g(7), wpg, wpp, i)
    return h.reshape(b, t, d)
```

```python
import functools
import math

import jax
import jax.numpy as jnp
import numpy as np
from jax import lax
from jax.experimental import pallas as pl
from jax.experimental.pallas import tpu as pltpu

F32 = jnp.float32
BF16 = jnp.bfloat16

D_MODEL = 2048
DEPTH = 4
PLE_DIM = 256
D_FF = 5632
MIX_W = 1024
EPS = 1e-6
HALF = 0.5

RW_HEAD = 64
RW_HEADS = 16
RW_DECAY_LORA = 64
RW_AAA_LORA = 64
RW_GATE_LORA = 160
RW_LNX_EPS = 64e-5
RW_COLS = 3 * MIX_W + RW_DECAY_LORA + RW_AAA_LORA + RW_GATE_LORA
RW_COLS_PAD = 3456
RW_GATE_PAD = RW_COLS_PAD - (3 * MIX_W + RW_DECAY_LORA + RW_AAA_LORA)

DSA_HEAD_DIM = 128
DSA_HEADS = 8
DSA_KV_RANK = 256
IDX_HEADS = 16
IDX_DIM = 64
TOPK_MAX = 256
DSA_COLS = DSA_HEADS * DSA_HEAD_DIM + DSA_KV_RANK + IDX_HEADS * IDX_DIM + IDX_DIM + IDX_HEADS

ML_HEADS = 4
ML_V = 256
ML_QK = 128
ML_CONV = 4
ML_CHUNK = 128
ML_COLS = 2 * ML_HEADS * ML_QK + 2 * MIX_W + 2 * ML_HEADS
ML_COLS_PAD = 3200

REL_BUCKETS = 32
REL_MAX_DIST = 128

VMEM_LIMIT_BYTES = 56 * 1024 * 1024

RW_CHUNK = 64
RW_BLOCK = 256


def _cparams(*sem):
    return pltpu.CompilerParams(dimension_semantics=sem, vmem_limit_bytes=VMEM_LIMIT_BYTES)


def _rms(x, g, eps=EPS):
    ms = jnp.mean(x * x, axis=-1, keepdims=True)
    return x * lax.rsqrt(ms + eps) * g


def _sigmoid(x):
    return 1.0 / (1.0 + jnp.exp(-x))


def _softplus(x):
    return jnp.maximum(x, 0.0) + jnp.log(1.0 + jnp.exp(-jnp.abs(x)))


def _split3(x):
    hi = x.astype(BF16)
    r1 = x - hi.astype(F32)
    mid = r1.astype(BF16)
    lo = (r1 - mid.astype(F32)).astype(BF16)
    return hi, mid, lo


def _dot(a, b):
    return jnp.dot(a.astype(BF16), b.astype(BF16), preferred_element_type=F32)


def _bdot(a, b, ca, cb):
    return lax.dot_general(a.astype(BF16), b.astype(BF16), (((ca,), (cb,)), ((0,), (0,))),
                           preferred_element_type=F32)


def _rwkv_kernel(z_ref, zprev_ref, mu_ref, w0_ref, wup_ref, a0_ref, aup_ref, gup_ref,
                 kk_ref, ka_ref, rk_ref, lng_ref, lnb_ref, tri_ref,
                 y_ref,
                 s_ref, at_ref, rt_ref, bt_ref, kt_ref, v_ref, wc_ref, y3_ref):
    i = pl.program_id(0)
    tb = z_ref.shape[0]
    nh, hd, c = RW_HEADS, RW_HEAD, RW_CHUNK
    nchunk = tb // c

    @pl.when(i == 0)
    def _():
        s_ref[...] = jnp.zeros_like(s_ref)

    z = z_ref[...]
    row = lax.broadcasted_iota(jnp.int32, (tb, 1), 0)
    prev_row = jnp.where(i == 0, 0.0, zprev_ref[7:8, :])
    shifted = jnp.where(row == 0, prev_row, pltpu.roll(z, 1, 0))
    zl = z + (shifted - z) * mu_ref[...]
    m = MIX_W
    r = zl[:, 0:m]
    k = zl[:, m:2 * m]
    v = zl[:, 2 * m:3 * m]
    wd = zl[:, 3 * m:3 * m + 64]
    ad = zl[:, 3 * m + 64:3 * m + 128]
    gd = zl[:, 3 * m + 128:]
    wpre = w0_ref[...] + _dot(jnp.tanh(wd), wup_ref[...])
    lw = -jnp.exp(-_softplus(-wpre) - 0.5)
    a = _sigmoid(a0_ref[...] + _dot(ad, aup_ref[...]))
    g = _dot(_sigmoid(gd), gup_ref[...])
    kkr = k * kk_ref[...]
    k2 = k * (1.0 + (a - 1.0) * ka_ref[...])
    rk2 = r * k2 * rk_ref[...]

    tri = tri_ref[...]
    h3, m3, l3 = _split3(lw)
    cum = (jnp.dot(tri, h3, preferred_element_type=F32) + jnp.dot(tri, m3, preferred_element_type=F32)
           + jnp.dot(tri, l3, preferred_element_type=F32))
    wc = jnp.exp(cum)
    winv = jnp.exp(-cum)
    wprev = jnp.exp(cum - lw)

    def stack(x):
        return jnp.stack([x[:, h * hd:(h + 1) * hd] for h in range(nh)], axis=0)

    kkr3 = stack(kkr)
    ss = jnp.sum(kkr3 * kkr3, axis=-1, keepdims=True)
    kk3 = kkr3 / jnp.maximum(jnp.sqrt(ss), 1e-12)
    a3 = stack(a)
    at_ref[...] = -kk3 * stack(wprev)
    winv3 = stack(winv)
    bt_ref[...] = kk3 * a3 * winv3
    kt_ref[...] = stack(k2) * winv3
    wc3 = stack(wc)
    wc_ref[...] = wc3
    r3 = stack(r)
    rt_ref[...] = r3 * wc3
    v3 = stack(v)
    v_ref[...] = v3

    ti = lax.broadcasted_iota(jnp.int32, (1, c, c), 1)
    si = lax.broadcasted_iota(jnp.int32, (1, c, c), 2)
    strict = si < ti
    incl = si <= ti
    eye = (si == ti).astype(F32)

    def chunk(ci, carry):
        sl = pl.ds(pl.multiple_of(ci * c, c), c)
        at = at_ref[:, sl, :]
        rt = rt_ref[:, sl, :]
        bt = bt_ref[:, sl, :]
        kt = kt_ref[:, sl, :]
        vv = v_ref[:, sl, :]
        wcl = wc_ref[:, pl.ds(ci * c + (c - 1), 1), :]
        ab = _bdot(at, bt, 2, 2)
        ak = _bdot(at, kt, 2, 2)
        rb = _bdot(rt, bt, 2, 2)
        rkm = _bdot(rt, kt, 2, 2)
        lmat = jnp.where(strict, ab, 0.0)
        aks = jnp.where(strict, ak, 0.0)
        rbt = jnp.where(incl, rb, 0.0)
        rkt = jnp.where(incl, rkm, 0.0)
        tinv = eye + lmat
        lp = lmat
        for _ in range(int(math.log2(c)) - 1):
            lp = _bdot(lp, lp, 2, 1)
            tinv = tinv + _bdot(lp, tinv, 2, 1)
        xv = _bdot(aks, vv, 2, 1)
        p = _bdot(tinv, at, 2, 1)
        qm = _bdot(tinv, xv, 2, 1)
        gm = rt + _bdot(rbt, p, 2, 1)
        y0 = _bdot(rbt, qm, 2, 1) + _bdot(rkt, vv, 2, 1)
        s = s_ref[...]
        u = _bdot(p, s, 2, 2) + qm
        y3_ref[:, sl, :] = _bdot(gm, s, 2, 2) + y0
        s_ref[...] = s * wcl + _bdot(u, bt * wcl, 1, 1) + _bdot(vv, kt * wcl, 1, 1)
        return carry

    lax.fori_loop(0, nchunk, chunk, 0)

    y3 = y3_ref[...]
    mean = jnp.mean(y3, axis=-1, keepdims=True)
    yc = y3 - mean
    var = jnp.mean(yc * yc, axis=-1, keepdims=True)
    yn3 = yc * lax.rsqrt(var + RW_LNX_EPS)
    bonus3 = jnp.sum(stack(rk2), axis=-1, keepdims=True) * v3
    yn = jnp.concatenate([yn3[h] for h in range(nh)], axis=-1)
    bonus = jnp.concatenate([bonus3[h] for h in range(nh)], axis=-1)
    y_ref[...] = ((yn * lng_ref[...] + lnb_ref[...] + bonus) * g).astype(y_ref.dtype)


def _rwkv_call(z_rw, mu, w0, w_up, a0, a_up, g_up, k_k, k_a, r_k, lnx_g, lnx_b):
    t = z_rw.shape[0]
    tb = min(RW_BLOCK, t)
    c = RW_CHUNK
    tri = jnp.asarray(np.kron(np.eye(tb // c), np.tril(np.ones((c, c)))), BF16)
    row = lambda n: pl.BlockSpec((1, n), lambda i: (0, 0))
    full = lambda a: pl.BlockSpec(a.shape, lambda i: (0,) * a.ndim)
    st = lambda: pltpu.VMEM((RW_HEADS, tb, RW_HEAD), F32)
    return pl.pallas_call(
        _rwkv_kernel,
        grid=(t // tb,),
        in_specs=[
            pl.BlockSpec((tb, RW_COLS_PAD), lambda i: (i, 0)),
            pl.BlockSpec((8, RW_COLS_PAD), lambda i: (jnp.maximum(i * (tb // 8) - 1, 0), 0)),
            row(RW_COLS_PAD), row(MIX_W), full(w_up), row(MIX_W), full(a_up), full(g_up),
            row(MIX_W), row(MIX_W), row(MIX_W), row(MIX_W), row(MIX_W), full(tri),
        ],
        out_specs=pl.BlockSpec((tb, MIX_W), lambda i: (i, 0)),
        out_shape=jax.ShapeDtypeStruct((t, MIX_W), BF16),
        scratch_shapes=[pltpu.VMEM((RW_HEADS, RW_HEAD, RW_HEAD), F32),
                        st(), st(), st(), st(), st(), st(), st()],
        compiler_params=_cparams("arbitrary"),
        name="rwkv7",
    )(z_rw, z_rw, mu, w0, w_up, a0, a_up, g_up, k_k, k_a, r_k, lnx_g, lnx_b, tri)


def _proj_tok_kernel(h_ref, g_ref, w_ref, o_ref, xn_ref):
    @pl.when(pl.program_id(1) == 0)
    def _():
        xn_ref[...] = _rms(h_ref[...], g_ref[...]).astype(BF16)

    o_ref[...] = jnp.dot(xn_ref[...], w_ref[...], preferred_element_type=F32).astype(o_ref.dtype)


def _proj_tok(h, g, w, layer, tn, out_dtype=F32, tm=1024):
    t, d = h.shape
    n = w.shape[2]
    tm = min(tm, t)
    return pl.pallas_call(
        _proj_tok_kernel,
        grid=(t // tm, n // tn),
        in_specs=[pl.BlockSpec((tm, d), lambda i, j: (i, 0)),
                  pl.BlockSpec((1, d), lambda i, j: (0, 0)),
                  pl.BlockSpec((None, d, tn), lambda i, j: (layer, 0, j))],
        out_specs=pl.BlockSpec((tm, tn), lambda i, j: (i, j)),
        out_shape=jax.ShapeDtypeStruct((t, n), out_dtype),
        scratch_shapes=[pltpu.VMEM((tm, d), BF16)],
        compiler_params=_cparams("parallel", "arbitrary"),
        name="proj_tok",
    )(h, g, w)


def _proj_feat_kernel(h_ref, g_ref, wt_ref, o_ref, xn_ref):
    @pl.when(pl.program_id(1) == 0)
    def _():
        xn_ref[...] = _rms(h_ref[...], g_ref[...]).astype(BF16)

    o_ref[...] = lax.dot_general(wt_ref[...], xn_ref[...], (((1,), (1,)), ((), ())),
                                 preferred_element_type=F32)


def _proj_feat(h, g, wt, layer, tn, tm=1024):
    t, d = h.shape
    n = wt.shape[1]
    tm = min(tm, t)
    return pl.pallas_call(
        _proj_feat_kernel,
        grid=(t // tm, n // tn),
        in_specs=[pl.BlockSpec((tm, d), lambda i, j: (i, 0)),
                  pl.BlockSpec((1, d), lambda i, j: (0, 0)),
                  pl.BlockSpec((None, tn, d), lambda i, j: (layer, j, 0))],
        out_specs=pl.BlockSpec((tn, tm), lambda i, j: (j, i)),
        out_shape=jax.ShapeDtypeStruct((n, t), F32),
        scratch_shapes=[pltpu.VMEM((tm, d), BF16)],
        compiler_params=_cparams("parallel", "arbitrary"),
        name="proj_feat",
    )(h, g, wt)


def _ffn_kernel(h_ref, gpre_ref, gpost_ref, wg_ref, wu_ref, wd_ref, o_ref, xn_ref, acc_ref):
    j = pl.program_id(1)

    @pl.when(j == 0)
    def _():
        xn_ref[...] = _rms(h_ref[...], gpre_ref[...]).astype(BF16)
        acc_ref[...] = jnp.zeros_like(acc_ref)

    xn = xn_ref[...]
    gate = jnp.dot(xn, wg_ref[...], preferred_element_type=F32)
    up = jnp.dot(xn, wu_ref[...], preferred_element_type=F32)
    act = (gate * _sigmoid(gate) * up).astype(BF16)
    acc_ref[...] += jnp.dot(act, wd_ref[...], preferred_element_type=F32)

    @pl.when(j == pl.num_programs(1) - 1)
    def _():
        o_ref[...] = h_ref[...] + HALF * _rms(acc_ref[...], gpost_ref[...])


def _ffn(h, gpre, gpost, wg, wu, wd, layer, tm=512, tf=512):
    t, d = h.shape
    f = wg.shape[2]
    tm = min(tm, t)
    return pl.pallas_call(
        _ffn_kernel,
        grid=(t // tm, f // tf),
        in_specs=[pl.BlockSpec((tm, d), lambda i, j: (i, 0)),
                  pl.BlockSpec((1, d), lambda i, j: (0, 0)),
                  pl.BlockSpec((1, d), lambda i, j: (0, 0)),
                  pl.BlockSpec((None, d, tf), lambda i, j: (layer, 0, j)),
                  pl.BlockSpec((None, d, tf), lambda i, j: (layer, 0, j)),
                  pl.BlockSpec((None, tf, d), lambda i, j: (layer, j, 0))],
        out_specs=pl.BlockSpec((tm, d), lambda i, j: (i, 0)),
        out_shape=jax.ShapeDtypeStruct((t, d), F32),
        scratch_shapes=[pltpu.VMEM((tm, d), BF16), pltpu.VMEM((tm, d), F32)],
        compiler_params=_cparams("parallel", "arbitrary"),
        name="ffn",
    )(h, gpre, gpost, wg, wu, wd)


def _branch_kernel(yr_ref, yd_ref, ym_ref, gr_ref, gd_ref, gm_ref, wb_ref, o_ref):
    acc = _sigmoid(gr_ref[...].astype(F32)) * jnp.dot(yr_ref[...], wb_ref[0], preferred_element_type=F32)
    acc += _sigmoid(gd_ref[...].astype(F32)) * jnp.dot(yd_ref[...], wb_ref[1], preferred_element_type=F32)
    acc += _sigmoid(gm_ref[...].astype(F32)) * jnp.dot(ym_ref[...], wb_ref[2], preferred_element_type=F32)
    o_ref[...] = acc.astype(o_ref.dtype)


def _branch_merge(y_rw, y_dsa, y_ml, z_gate, wb, layer, tm=1024, tn=512):
    t = y_rw.shape[0]
    d = D_MODEL
    tm = min(tm, t)
    nj = d // tn
    yspec = pl.BlockSpec((tm, MIX_W), lambda i, j: (i, 0))
    gspec = lambda b: pl.BlockSpec((tm, tn), lambda i, j: (i, b * nj + j))
    return pl.pallas_call(
        _branch_kernel,
        grid=(t // tm, nj),
        in_specs=[yspec, yspec, yspec, gspec(0), gspec(1), gspec(2),
                  pl.BlockSpec((None, 3, MIX_W, tn), lambda i, j: (layer, 0, 0, j))],
        out_specs=pl.BlockSpec((tm, tn), lambda i, j: (i, j)),
        out_shape=jax.ShapeDtypeStruct((t, d), BF16),
        compiler_params=_cparams("parallel", "arbitrary"),
        name="branch_merge",
    )(y_rw, y_dsa, y_ml, z_gate, z_gate, z_gate, wb)


def _outproj_kernel(h_ref, x_ref, g_ref, w_ref, o_ref):
    mix = jnp.dot(x_ref[...], w_ref[...], preferred_element_type=F32)
    o_ref[...] = h_ref[...] + _rms(mix, g_ref[...])


def _outproj(h, x, g, w, layer, tm=512):
    t, d = h.shape
    tm = min(tm, t)
    return pl.pallas_call(
        _outproj_kernel,
        grid=(t // tm,),
        in_specs=[pl.BlockSpec((tm, d), lambda i: (i, 0)),
                  pl.BlockSpec((tm, d), lambda i: (i, 0)),
                  pl.BlockSpec((1, d), lambda i: (0, 0)),
                  pl.BlockSpec((None, d, d), lambda i: (layer, 0, 0))],
        out_specs=pl.BlockSpec((tm, d), lambda i: (i, 0)),
        out_shape=jax.ShapeDtypeStruct((t, d), F32),
        compiler_params=_cparams("parallel"),
        name="outproj",
    )(h, x, g, w)


def _ple_kernel(h_ref, p_ref, g6_ref, g7_ref, wg_ref, wp_ref, o_ref):
    h = h_ref[...]
    pg = _sigmoid(jnp.dot(_rms(h, g6_ref[...]).astype(BF16), wg_ref[...], preferred_element_type=F32))
    e = jnp.dot(p_ref[...].astype(BF16), wp_ref[...], preferred_element_type=F32)
    o_ref[...] = h + _rms(pg * e, g7_ref[...])


def _ple(h, p, g6, g7, wg, wp, layer, tm=512):
    t, d = h.shape
    tm = min(tm, t)
    return pl.pallas_call(
        _ple_kernel,
        grid=(t // tm,),
        in_specs=[pl.BlockSpec((tm, d), lambda i: (i, 0)),
                  pl.BlockSpec((None, tm, PLE_DIM), lambda i: (layer, i, 0)),
                  pl.BlockSpec((1, d), lambda i: (0, 0)),
                  pl.BlockSpec((1, d), lambda i: (0, 0)),
                  pl.BlockSpec((None, d, d), lambda i: (layer, 0, 0)),
                  pl.BlockSpec((None, PLE_DIM, d), lambda i: (layer, 0, 0))],
        out_specs=pl.BlockSpec((tm, d), lambda i: (i, 0)),
        out_shape=jax.ShapeDtypeStruct((t, d), F32),
        compiler_params=_cparams("parallel"),
        name="ple",
    )(h, p, g6, g7, wg, wp)


def _log_sigmoid(x):
    return jnp.minimum(x, 0.0) - jnp.log(1.0 + jnp.exp(-jnp.abs(x)))


def _mlstm_kernel(z_ref, gt_ref, conv_ref, bcol_ref, brow_ref, hng_ref, tri_ref, y_ref,
                  c_ref, n_ref, m_ref, prev_ref):
    i = pl.program_id(0)
    ln = z_ref.shape[0]
    nh, dk, dv = ML_HEADS, ML_QK, ML_V
    nqk = 2 * nh * dk

    @pl.when(i == 0)
    def _():
        c_ref[...] = jnp.zeros_like(c_ref)
        n_ref[...] = jnp.zeros_like(n_ref)
        m_ref[...] = jnp.zeros_like(m_ref)
        prev_ref[...] = jnp.zeros_like(prev_ref)

    x = z_ref[:, 0:nqk]
    prev = prev_ref[...]
    row8 = lax.broadcasted_iota(jnp.int32, (8, 1), 0)
    w = conv_ref[...]
    acc = x * w[ML_CONV - 1:ML_CONV, :]
    for d in range(1, ML_CONV):
        sh = pltpu.roll(x, d, 0)
        top = jnp.where(row8 < d, pltpu.roll(prev, d, 0), sh[0:8, :])
        sh = jnp.concatenate([top, sh[8:, :]], axis=0)
        acc = acc + sh * w[ML_CONV - 1 - d:ML_CONV - d, :]
    prev_ref[...] = x[ln - 8:ln, :]
    qk = acc * _sigmoid(acc)
    q = qk[:, 0:nh * dk] * (dk ** -0.5)
    k = qk[:, nh * dk:nqk]

    gcol = z_ref[:, nqk + 2 * MIX_W:nqk + 2 * MIX_W + 2 * nh] + bcol_ref[...]
    grow = gt_ref[...] + brow_ref[...]
    lcol = _log_sigmoid(gcol)
    lrow = _log_sigmoid(grow)
    tri = tri_ref[...]
    c3 = _split3(lcol)
    cum_col = sum(jnp.dot(tri, t3, preferred_element_type=F32) for t3 in c3)
    r3 = _split3(lrow)
    cum_row = sum(lax.dot_general(t3, tri, (((1,), (1,)), ((), ())), preferred_element_type=F32) for t3 in r3)

    ti = lax.broadcasted_iota(jnp.int32, (ln, ln), 0)
    si = lax.broadcasted_iota(jnp.int32, (ln, ln), 1)
    tril = si <= ti
    for h in range(nh):
        g_col = cum_col[:, nh + h:nh + h + 1]
        g_row = cum_row[nh + h:nh + h + 1, :]
        i_col = gcol[:, h:h + 1]
        i_row = grow[h:h + 1, :]
        m_st = m_ref[h, 0:1, 0:1]
        c_st = c_ref[h]
        n_st = n_ref[h, 0:1, :]
        qh = q[:, h * dk:(h + 1) * dk]
        kh = k[:, h * dk:(h + 1) * dk]
        vh = z_ref[:, nqk + h * dv:nqk + (h + 1) * dv]
        oh = z_ref[:, nqk + MIX_W + h * dv:nqk + MIX_W + (h + 1) * dv]
        a_log = g_col + m_st
        d_log = jnp.where(tril, g_col - g_row + i_row, -jnp.inf)
        m_q = jnp.maximum(a_log, jnp.max(d_log, axis=-1, keepdims=True))
        inter = jnp.exp(a_log - m_q)
        s_w = lax.dot_general(qh.astype(BF16), kh.astype(BF16), (((1,), (1,)), ((), ())),
                              preferred_element_type=F32) * jnp.exp(d_log - m_q)
        num = inter * _dot(qh, c_st) + _dot(s_w, vh)
        den = inter * jnp.sum(qh * n_st, axis=-1, keepdims=True) + jnp.sum(s_w, axis=-1, keepdims=True)
        hh = num / jnp.maximum(jnp.abs(den), jnp.exp(-m_q))
        g_tot = g_col[ln - 1:ln, :]
        w_log = g_tot - g_col + i_col
        m_new = jnp.maximum(g_tot + m_st, jnp.max(w_log, axis=0, keepdims=True))
        decay = jnp.exp(g_tot + m_st - m_new)
        kw = kh * jnp.exp(w_log - m_new)
        c_ref[h] = decay * c_st + lax.dot_general(kw.astype(BF16), vh.astype(BF16), (((0,), (0,)), ((), ())),
                                                  preferred_element_type=F32)
        n_ref[h] = jnp.broadcast_to(decay * n_st + jnp.sum(kw, axis=0, keepdims=True), (8, dk))
        m_ref[h] = jnp.broadcast_to(m_new, (8, 128))
        hn = hh * lax.rsqrt(jnp.mean(hh * hh, axis=-1, keepdims=True) + EPS) * hng_ref[:, h * dv:(h + 1) * dv]
        y_ref[:, h * dv:(h + 1) * dv] = (_sigmoid(oh) * hn).astype(y_ref.dtype)


def _mlstm_call(z_ml, zt, gate_row0, conv_w, b_col, b_row, hn_g):
    t = z_ml.shape[0]
    ln = ML_CHUNK
    tri = jnp.asarray(np.tril(np.ones((ln, ln))), BF16)
    full = lambda a: pl.BlockSpec(a.shape, lambda i: (0,) * a.ndim)
    return pl.pallas_call(
        _mlstm_kernel,
        grid=(t // ln,),
        in_specs=[pl.BlockSpec((ln, ML_COLS_PAD), lambda i: (i, 0)),
                  pl.BlockSpec((8, ln), lambda i: (gate_row0 // 8, i)),
                  full(conv_w), full(b_col), full(b_row), full(hn_g), full(tri)],
        out_specs=pl.BlockSpec((ln, MIX_W), lambda i: (i, 0)),
        out_shape=jax.ShapeDtypeStruct((t, MIX_W), BF16),
        scratch_shapes=[pltpu.VMEM((ML_HEADS, ML_QK, ML_V), F32),
                        pltpu.VMEM((ML_HEADS, 8, ML_QK), F32),
                        pltpu.VMEM((ML_HEADS, 8, 128), F32),
                        pltpu.VMEM((8, 2 * ML_HEADS * ML_QK), F32)],
        compiler_params=_cparams("arbitrary"),
        name="mlstm",
    )(z_ml, zt, conv_w, b_col, b_row, hn_g, tri)


DSA_TQ = 256
DSA_GROUP = 4
INT_MIN = -2 ** 31
NEG = -1e30
LOG2E = math.log2(math.e)
SHIFT_SLACK = 8.0


def _t5_bucket_bounds():
    max_exact = REL_BUCKETS // 2
    d = np.arange(1, 4 * REL_MAX_DIST, dtype=np.float64)
    large = max_exact + np.floor(np.log(d / max_exact) / math.log(REL_MAX_DIST / max_exact)
                                 * (REL_BUCKETS - max_exact)).astype(np.int64)
    bucket = np.where(d < max_exact, d.astype(np.int64), np.minimum(large, REL_BUCKETS - 1))
    return [int(d[np.argmax(bucket >= b)]) for b in range(max_exact + 1, REL_BUCKETS)]


def _bias_tile_kernel(rb_ref, o_ref):
    o = pl.program_id(0)
    h = pl.program_id(1)
    tq = o_ref.shape[-1]
    s_loc = lax.broadcasted_iota(jnp.int32, (tq, tq), 0)
    t_loc = lax.broadcasted_iota(jnp.int32, (tq, tq), 1)
    dist = o * tq + t_loc - s_loc
    max_exact = REL_BUCKETS // 2
    bucket = jnp.full((tq, tq), max_exact, jnp.int32)
    for bnd in _t5_bucket_bounds():
        bucket = bucket + (dist >= bnd).astype(jnp.int32)
    bucket = jnp.where(dist < max_exact, jnp.maximum(dist, 0), bucket)
    bias = jnp.zeros((tq, tq), F32)
    for b in range(REL_BUCKETS):
        bias = jnp.where(bucket == b, rb_ref[b, h], bias)
    o_ref[...] = (bias - rb_ref[REL_BUCKETS - 1, h]) * LOG2E


def _bias_tiles(rel_bias, tq):
    assert tq >= REL_MAX_DIST
    return pl.pallas_call(
        _bias_tile_kernel,
        grid=(3, DSA_HEADS),
        in_specs=[pl.BlockSpec(memory_space=pltpu.SMEM)],
        out_specs=pl.BlockSpec((None, None, tq, tq), lambda o, h: (o, h, 0, 0)),
        out_shape=jax.ShapeDtypeStruct((3, DSA_HEADS, tq, tq), F32),
        name="t5_bias_tiles",
    )(rel_bias)


def _dsa_prep_kernel(zt_ref, ckvt_ref, g_ref, gcol_ref, gk_ref, ck_ref, ki_ref, ct_ref):
    ckv = zt_ref[:, 0:DSA_KV_RANK]
    ck_ref[...] = _rms(ckv, g_ref[...]).astype(BF16)
    ki_ref[...] = _rms(zt_ref[:, DSA_KV_RANK:DSA_KV_RANK + IDX_DIM], gk_ref[...]).astype(BF16)
    ct = ckvt_ref[...]
    ms = jnp.mean(ct * ct, axis=0, keepdims=True)
    ct_ref[...] = (ct * lax.rsqrt(ms + EPS) * gcol_ref[...]).astype(BF16)


def _dsa_prep(z_dtok, zt, ckv_row0, g_row, g_col, gk_row):
    t = z_dtok.shape[0]
    tq = DSA_TQ
    full = lambda a: pl.BlockSpec(a.shape, lambda i: (0,) * a.ndim)
    return pl.pallas_call(
        _dsa_prep_kernel,
        grid=(t // tq,),
        in_specs=[pl.BlockSpec((tq, z_dtok.shape[1]), lambda i: (i, 0)),
                  pl.BlockSpec((DSA_KV_RANK, tq), lambda i: (ckv_row0 // DSA_KV_RANK, i)),
                  full(g_row), full(g_col), full(gk_row)],
        out_specs=[pl.BlockSpec((tq, DSA_KV_RANK), lambda i: (i, 0)),
                   pl.BlockSpec((tq, IDX_DIM), lambda i: (i, 0)),
                   pl.BlockSpec((None, DSA_KV_RANK, tq), lambda i: (i, 0, 0))],
        out_shape=[jax.ShapeDtypeStruct((t, DSA_KV_RANK), BF16),
                   jax.ShapeDtypeStruct((t, IDX_DIM), BF16),
                   jax.ShapeDtypeStruct((t // tq, DSA_KV_RANK, tq), BF16)],
        compiler_params=_cparams("parallel"),
        name="dsa_prep",
    )(z_dtok, zt, g_row, g_col, gk_row)


def _dsa_kernel(topk, qt_ref, qit_ref, wt_ref, ki_ref, ck_ref, ct_ref, wuk_ref, wuv_ref, bias_ref,
                o_ref, key_ref, acc_ref, m_ref, l_ref, qlat_ref, qi_ref, s_ref, tmax_ref):
    i = pl.program_id(0)
    tq = DSA_TQ
    nkb = i + 1
    nh, hd, r = DSA_HEADS, DSA_HEAD_DIM, DSA_KV_RANK

    for h in range(nh):
        ql = jnp.dot(wuk_ref[h], qt_ref[h * hd:(h + 1) * hd, :].astype(BF16), preferred_element_type=F32)
        qlat_ref[h] = (ql * (hd ** -0.5 * LOG2E)).astype(BF16)
    qi_ref[...] = qit_ref[...].astype(BF16)
    wrow = wt_ref[...] * ((IDX_DIM ** -0.5) * (IDX_HEADS ** -0.5))
    s_loc = lax.broadcasted_iota(jnp.int32, (tq, tq), 0)
    t_loc = lax.broadcasted_iota(jnp.int32, (tq, tq), 1)

    def score_blk(kb, carry):
        kblk = ki_ref[pl.ds(pl.multiple_of(kb * tq, tq), tq), :]
        acc = jnp.zeros((tq, tq), F32)
        for h in range(IDX_HEADS):
            rel = jnp.dot(kblk, qi_ref[h * IDX_DIM:(h + 1) * IDX_DIM, :], preferred_element_type=F32)
            acc = acc + wrow[h:h + 1, :] * jnp.maximum(rel, 0.0)
        bits = pltpu.bitcast(acc, jnp.int32)
        key = jnp.where(bits < 0, bits ^ 0x7FFFFFFF, bits)
        causal = (kb * tq + s_loc) <= (i * tq + t_loc)
        key_ref[kb] = jnp.where(causal, key, INT_MIN)
        return carry

    lax.fori_loop(0, nkb, score_blk, 0)

    def count_ge(cand):
        def body(kb, cnt):
            ge = (key_ref[kb] >= cand).astype(jnp.int32)
            return cnt + jnp.sum(ge.reshape(tq // 8, 8, tq), axis=0)
        cnt = lax.fori_loop(0, nkb, body, jnp.zeros((8, tq), jnp.int32))
        return jnp.sum(cnt, axis=0, keepdims=True)

    few = (i * tq + t_loc[0:1, :]) < topk

    def search_cond(c):
        return jnp.logical_and(c[0] < 32, c[3] > 0)

    def search_body(c):
        step, cur, cnt, _ = c
        cand = cur + jnp.left_shift(jnp.int32(1), 31 - step)
        cnt_c = count_ge(cand)
        take = cnt_c >= topk
        cur = jnp.where(take, cand, cur)
        cnt = jnp.where(take, cnt_c, cnt)
        pending = jnp.sum(jnp.where(few, 0, (cnt != topk).astype(jnp.int32)))
        return step + 1, cur, cnt, pending

    init = (jnp.int32(0), jnp.full((1, tq), INT_MIN, jnp.int32), jnp.full((1, tq), nkb * tq, jnp.int32),
            jnp.int32(1))
    thr = lax.while_loop(search_cond, search_body, init)[1]
    thr = jnp.where(few, INT_MIN + 1, jnp.maximum(thr, INT_MIN + 1))

    m_ref[...] = jnp.full_like(m_ref, NEG)
    l_ref[...] = jnp.zeros_like(l_ref)
    acc_ref[...] = jnp.zeros_like(acc_ref)

    grp = DSA_GROUP
    ngrp = (i + grp) // grp

    def pad_blk(kb, carry):
        key_ref[kb] = jnp.full((tq, tq), INT_MIN, jnp.int32)
        return carry

    lax.fori_loop(nkb, ngrp * grp, pad_blk, 0)

    def att_grp(near, g, carry):
        excess = jnp.full((1, tq), NEG, F32)
        tmax = [None] * nh
        for j in range(grp):
            kb = g * grp + j
            addm = jnp.where(key_ref[kb] >= thr, 0.0, NEG)
            cblk = ck_ref[pl.ds(pl.multiple_of(kb * tq, tq), tq), :]
            for h in range(nh):
                s = jnp.dot(cblk, qlat_ref[h], preferred_element_type=F32) + addm
                if near:
                    s = s + bias_ref[jnp.clip(i - kb, 0, 2), h]
                s_ref[j * nh + h] = s
                tm = jnp.max(s, axis=0, keepdims=True)
                tmax[h] = tm if j == 0 else jnp.maximum(tmax[h], tm)
        for h in range(nh):
            tmax_ref[h] = tmax[h]
            excess = jnp.maximum(excess, tmax[h] - m_ref[h])

        @pl.when(jnp.max(excess) > SHIFT_SLACK)
        def _():
            for h in range(nh):
                m_old = m_ref[h]
                m_new = jnp.maximum(m_old, tmax_ref[h])
                corr = jnp.exp2(m_old - m_new)
                l_ref[h] = l_ref[h] * corr
                acc_ref[h] = acc_ref[h] * corr
                m_ref[h] = m_new

        ctg = jnp.concatenate([ct_ref[g * grp + j] for j in range(grp)], axis=1)
        for h in range(nh):
            m_h = m_ref[h]
            p = jnp.concatenate([jnp.exp2(s_ref[j * nh + h] - m_h) for j in range(grp)], axis=0)
            l_ref[h] = l_ref[h] + jnp.sum(p, axis=0, keepdims=True)
            acc_ref[h] = acc_ref[h] + jnp.dot(ctg, p.astype(BF16), preferred_element_type=F32)
        return carry

    nfar = jnp.maximum(i - 1, 0) // grp
    lax.fori_loop(0, nfar, functools.partial(att_grp, False), 0)
    lax.fori_loop(nfar, ngrp, functools.partial(att_grp, True), 0)

    for h in range(nh):
        o_lat = (acc_ref[h] / l_ref[h]).astype(BF16)
        out_t = jnp.dot(wuv_ref[h], o_lat, preferred_element_type=F32)
        o_ref[:, h * hd:(h + 1) * hd] = out_t.T.astype(o_ref.dtype)


def _dsa_call(zt, q_row0, qi_row0, w_row0, ki, ck, ct, wuk_t, wuv_t, bias_tiles, topk):
    t = zt.shape[1]
    tq = DSA_TQ
    nq = DSA_HEADS * DSA_HEAD_DIM
    ni = IDX_HEADS * IDX_DIM
    full = lambda a: pl.BlockSpec(a.shape, lambda i: (0,) * a.ndim, pipeline_mode=pl.Buffered(1))
    return pl.pallas_call(
        functools.partial(_dsa_kernel, topk),
        grid=(t // tq,),
        in_specs=[pl.BlockSpec((nq, tq), lambda i: (q_row0 // nq, i)),
                  pl.BlockSpec((ni, tq), lambda i: (qi_row0 // ni, i)),
                  pl.BlockSpec((IDX_HEADS, tq), lambda i: (w_row0 // IDX_HEADS, i)),
                  full(ki), full(ck), full(ct), full(wuk_t), full(wuv_t), full(bias_tiles)],
        out_specs=pl.BlockSpec((tq, MIX_W), lambda i: (i, 0)),
        out_shape=jax.ShapeDtypeStruct((t, MIX_W), BF16),
        scratch_shapes=[pltpu.VMEM((t // tq, tq, tq), jnp.int32),
                        pltpu.VMEM((DSA_HEADS, DSA_KV_RANK, tq), F32),
                        pltpu.VMEM((DSA_HEADS, 1, tq), F32),
                        pltpu.VMEM((DSA_HEADS, 1, tq), F32),
                        pltpu.VMEM((DSA_HEADS, DSA_KV_RANK, tq), BF16),
                        pltpu.VMEM((ni, tq), BF16),
                        pltpu.VMEM((DSA_GROUP * DSA_HEADS, tq, tq), F32),
                        pltpu.VMEM((DSA_HEADS, 1, tq), F32)],
        compiler_params=_cparams("arbitrary"),
        name="dsa",
    )(zt, zt, zt, ki, ck, ct, wuk_t, wuv_t, bias_tiles)


FEAT_Q0 = 0
FEAT_QI0 = FEAT_Q0 + DSA_HEADS * DSA_HEAD_DIM
FEAT_CKV0 = FEAT_QI0 + IDX_HEADS * IDX_DIM
FEAT_W0 = FEAT_CKV0 + DSA_KV_RANK
FEAT_MLG0 = FEAT_W0 + IDX_HEADS
FEAT_ROWS = 2560
DTOK_COLS = 384


def _pad_last(a, n):
    return jnp.pad(a, [(0, 0)] * (a.ndim - 1) + [(0, n - a.shape[-1])])


def kernel(x, p, norm_gains, ffn1_gate, ffn1_up, ffn1_down, w_in, rw_mu, rw_w0, rw_w_up, rw_a0, rw_a_up, rw_g_up, rw_k_k, rw_k_a, rw_r_k, rw_lnx_g, rw_lnx_b, dsa_ckv_g, dsa_kidx_g, dsa_w_uk, dsa_w_uv, rel_bias, ml_conv, ml_b_i, ml_b_f, ml_hn_g, w_branch, w_out, ffn2_gate, ffn2_up, ffn2_down, ple_proj, ple_gate):
    b, t, d = x.shape
    assert b == 1 and d == D_MODEL
    nl = w_in.shape[0]
    h = x.reshape(t, d)
    p = p.reshape(nl, t, PLE_DIM)
    bf = lambda a: a.astype(BF16)

    f1g, f1u, f1d = bf(ffn1_gate), bf(ffn1_up), bf(ffn1_down)
    f2g, f2u, f2d = bf(ffn2_gate), bf(ffn2_up), bf(ffn2_down)
    wb, wo, wpg, wpp = bf(w_branch), bf(w_out), bf(ple_gate), bf(ple_proj)
    o_dsa = RW_COLS
    o_ml = o_dsa + DSA_COLS
    o_gate = o_ml + ML_COLS
    w_rw = bf(_pad_last(w_in[:, :, 0:RW_COLS], RW_COLS_PAD))
    w_ml = bf(_pad_last(w_in[:, :, o_ml:o_gate], ML_COLS_PAD))
    w_gate = bf(w_in[:, :, o_gate:])
    nq = DSA_HEADS * DSA_HEAD_DIM
    ni = IDX_HEADS * IDX_DIM
    w_q = w_in[:, :, o_dsa:o_dsa + nq]
    w_ckv = w_in[:, :, o_dsa + nq:o_dsa + nq + DSA_KV_RANK]
    w_qi = w_in[:, :, o_dsa + nq + DSA_KV_RANK:o_dsa + nq + DSA_KV_RANK + ni]
    w_ki = w_in[:, :, o_dsa + nq + DSA_KV_RANK + ni:o_dsa + nq + DSA_KV_RANK + ni + IDX_DIM]
    w_wi = w_in[:, :, o_dsa + DSA_COLS - IDX_HEADS:o_dsa + DSA_COLS]
    w_mlg = w_in[:, :, o_gate - 2 * ML_HEADS:o_gate]
    w_dtok = bf(_pad_last(jnp.concatenate([w_ckv, w_ki], axis=-1), DTOK_COLS))
    w_feat = bf(jnp.swapaxes(_pad_last(jnp.concatenate([w_q, w_qi, w_ckv, w_wi, w_mlg], axis=-1), FEAT_ROWS), 1, 2))
    mu = _pad_last(rw_mu, RW_COLS_PAD)
    wup, aup = bf(rw_w_up), bf(rw_a_up)
    gup = bf(jnp.pad(rw_g_up, ((0, 0), (0, RW_GATE_PAD - RW_GATE_LORA), (0, 0))))
    wuk_t = bf(jnp.swapaxes(dsa_w_uk, 2, 3))
    wuv_t = bf(jnp.swapaxes(dsa_w_uv, 2, 3))
    ml_b = jnp.concatenate([ml_b_i, ml_b_f], axis=-1)

    bias_tiles = _bias_tiles(rel_bias, DSA_TQ)
    topk = min(TOPK_MAX, t // 4)
    row = lambda a: a.reshape(1, -1)

    for i in range(nl):
        g = lambda k: norm_gains[i, k].reshape(1, d)
        h = _ffn(h, g(0), g(1), f1g, f1u, f1d, i)

        z_rw = _proj_tok(h, g(2), w_rw, i, tn=1152)
        z_ml = _proj_tok(h, g(2), w_ml, i, tn=640)
        z_gate = _proj_tok(h, g(2), w_gate, i, tn=1024, out_dtype=BF16)
        z_dtok = _proj_tok(h, g(2), w_dtok, i, tn=DTOK_COLS)
        zt = _proj_feat(h, g(2), w_feat, i, tn=1280)

        y_rw = _rwkv_call(z_rw, row(mu[i]), row(rw_w0[i]), wup[i], row(rw_a0[i]), aup[i], gup[i],
                          row(rw_k_k[i]), row(rw_k_a[i]), row(rw_r_k[i]), row(rw_lnx_g[i]), row(rw_lnx_b[i]))
        ck, ki, ct = _dsa_prep(z_dtok, zt, FEAT_CKV0, row(dsa_ckv_g[i]), dsa_ckv_g[i].reshape(-1, 1),
                               row(dsa_kidx_g[i]))
        y_dsa = _dsa_call(zt, FEAT_Q0, FEAT_QI0, FEAT_W0, ki, ck, ct, wuk_t[i], wuv_t[i], bias_tiles, topk)
        y_ml = _mlstm_call(z_ml, zt, FEAT_MLG0, ml_conv[i], row(ml_b[i]), ml_b[i].reshape(-1, 1), row(ml_hn_g[i]))

        mixsum = _branch_merge(y_rw, y_dsa, y_ml, z_gate, wb, i)
        h = _outproj(h, mixsum, g(3), wo, i)
        h = _ffn(h, g(4), g(5), f2g, f2u, f2d, i)
        h = _ple(h, p, g(6), g(7), wpg, wpp, i)
    return h.reshape(b, t, d)
```

```python
import functools
import math

import jax
import jax.numpy as jnp
import numpy as np
from jax import lax
from jax.experimental import pallas as pl
from jax.experimental.pallas import tpu as pltpu

F32 = jnp.float32
BF16 = jnp.bfloat16

D_MODEL = 2048
DEPTH = 4
PLE_DIM = 256
D_FF = 5632
MIX_W = 1024
EPS = 1e-6
HALF = 0.5

RW_HEAD = 64
RW_HEADS = 16
RW_DECAY_LORA = 64
RW_AAA_LORA = 64
RW_GATE_LORA = 160
RW_LNX_EPS = 64e-5
RW_COLS = 3 * MIX_W + RW_DECAY_LORA + RW_AAA_LORA + RW_GATE_LORA
RW_COLS_PAD = 3456
RW_GATE_PAD = RW_COLS_PAD - (3 * MIX_W + RW_DECAY_LORA + RW_AAA_LORA)

DSA_HEAD_DIM = 128
DSA_HEADS = 8
DSA_KV_RANK = 256
IDX_HEADS = 16
IDX_DIM = 64
TOPK_MAX = 256
DSA_COLS = DSA_HEADS * DSA_HEAD_DIM + DSA_KV_RANK + IDX_HEADS * IDX_DIM + IDX_DIM + IDX_HEADS

ML_HEADS = 4
ML_V = 256
ML_QK = 128
ML_CONV = 4
ML_CHUNK = 128
ML_COLS = 2 * ML_HEADS * ML_QK + 2 * MIX_W + 2 * ML_HEADS
ML_COLS_PAD = 3200

REL_BUCKETS = 32
REL_MAX_DIST = 128

VMEM_LIMIT_BYTES = 56 * 1024 * 1024

RW_CHUNK = 64
RW_BLOCK = 256


def _cparams(*sem):
    return pltpu.CompilerParams(dimension_semantics=sem, vmem_limit_bytes=VMEM_LIMIT_BYTES)


def _rms(x, g, eps=EPS):
    ms = jnp.mean(x * x, axis=-1, keepdims=True)
    return x * lax.rsqrt(ms + eps) * g


def _sigmoid(x):
    return 1.0 / (1.0 + jnp.exp(-x))


def _softplus(x):
    return jnp.maximum(x, 0.0) + jnp.log(1.0 + jnp.exp(-jnp.abs(x)))


def _split3(x):
    hi = x.astype(BF16)
    r1 = x - hi.astype(F32)
    mid = r1.astype(BF16)
    lo = (r1 - mid.astype(F32)).astype(BF16)
    return hi, mid, lo


def _dot(a, b):
    return jnp.dot(a.astype(BF16), b.astype(BF16), preferred_element_type=F32)


def _bdot(a, b, ca, cb):
    return lax.dot_general(a.astype(BF16), b.astype(BF16), (((ca,), (cb,)), ((0,), (0,))),
                           preferred_element_type=F32)


def _rwkv_kernel(z_ref, zprev_ref, mu_ref, w0_ref, wup_ref, a0_ref, aup_ref, gup_ref,
                 kk_ref, ka_ref, rk_ref, lng_ref, lnb_ref, tri_ref, bd_ref,
                 y_ref,
                 s_ref, at_ref, rt_ref, bt_ref, kt_ref, v_ref, wc_ref, y3_ref):
    i = pl.program_id(0)
    tb = z_ref.shape[0]
    nh, hd, c = RW_HEADS, RW_HEAD, RW_CHUNK
    nchunk = tb // c

    @pl.when(i == 0)
    def _():
        s_ref[...] = jnp.zeros_like(s_ref)

    z = z_ref[...]
    row = lax.broadcasted_iota(jnp.int32, (tb, 1), 0)
    prev_row = jnp.where(i == 0, 0.0, zprev_ref[7:8, :])
    shifted = jnp.where(row == 0, prev_row, pltpu.roll(z, 1, 0))
    zl = z + (shifted - z) * mu_ref[...]
    m = MIX_W
    r = zl[:, 0:m]
    k = zl[:, m:2 * m]
    v = zl[:, 2 * m:3 * m]
    wd = zl[:, 3 * m:3 * m + 64]
    ad = zl[:, 3 * m + 64:3 * m + 128]
    gd = zl[:, 3 * m + 128:]
    wpre = w0_ref[...] + _dot(jnp.tanh(wd), wup_ref[...])
    lw = -jnp.exp(-_softplus(-wpre) - 0.5)
    a = _sigmoid(a0_ref[...] + _dot(ad, aup_ref[...]))
    g = _dot(_sigmoid(gd), gup_ref[...])
    kkr = k * kk_ref[...]
    k2 = k * (1.0 + (a - 1.0) * ka_ref[...])
    rk2 = r * k2 * rk_ref[...]

    tri = tri_ref[...]
    h3, m3, l3 = _split3(lw)
    cum = (jnp.dot(tri, h3, preferred_element_type=F32) + jnp.dot(tri, m3, preferred_element_type=F32)
           + jnp.dot(tri, l3, preferred_element_type=F32))
    wc = jnp.exp(cum)
    winv = jnp.exp(-cum)
    wprev = jnp.exp(cum - lw)

    def stack(x):
        return jnp.stack([x[:, h * hd:(h + 1) * hd] for h in range(nh)], axis=0)

    bd = bd_ref[...]
    nb = bd.shape[0]

    def headsum(x):
        hi = x.astype(BF16)
        lo = (x - hi.astype(F32)).astype(BF16)
        return jnp.concatenate(
            [jnp.dot(hi[:, j:j + nb], bd, preferred_element_type=F32)
             + jnp.dot(lo[:, j:j + nb], bd, preferred_element_type=F32) for j in range(0, m, nb)], axis=1)

    kk = kkr / jnp.maximum(jnp.sqrt(headsum(kkr * kkr)), 1e-12)
    at_ref[...] = stack(-kk * wprev)
    bt_ref[...] = stack(kk * a * winv)
    kt_ref[...] = stack(k2 * winv)
    wc_ref[...] = stack(wc)
    rt_ref[...] = stack(r * wc)
    v_ref[...] = stack(v)

    ti = lax.broadcasted_iota(jnp.int32, (1, c, c), 1)
    si = lax.broadcasted_iota(jnp.int32, (1, c, c), 2)
    strict = si < ti
    incl = si <= ti
    eye = (si == ti).astype(F32)

    def chunk(ci, carry):
        sl = pl.ds(pl.multiple_of(ci * c, c), c)
        at = at_ref[:, sl, :]
        rt = rt_ref[:, sl, :]
        bt = bt_ref[:, sl, :]
        kt = kt_ref[:, sl, :]
        vv = v_ref[:, sl, :]
        wcl = wc_ref[:, pl.ds(ci * c + (c - 1), 1), :]
        ab = _bdot(at, bt, 2, 2)
        ak = _bdot(at, kt, 2, 2)
        rb = _bdot(rt, bt, 2, 2)
        rkm = _bdot(rt, kt, 2, 2)
        lmat = jnp.where(strict, ab, 0.0)
        aks = jnp.where(strict, ak, 0.0)
        rbt = jnp.where(incl, rb, 0.0)
        rkt = jnp.where(incl, rkm, 0.0)
        tinv = eye + lmat
        lp = lmat
        for _ in range(int(math.log2(c)) - 1):
            lp = _bdot(lp, lp, 2, 1)
            tinv = tinv + _bdot(lp, tinv, 2, 1)
        xv = _bdot(aks, vv, 2, 1)
        p = _bdot(tinv, at, 2, 1)
        qm = _bdot(tinv, xv, 2, 1)
        gm = rt + _bdot(rbt, p, 2, 1)
        y0 = _bdot(rbt, qm, 2, 1) + _bdot(rkt, vv, 2, 1)
        s = s_ref[...]
        u = _bdot(p, s, 2, 2) + qm
        y3_ref[:, sl, :] = _bdot(gm, s, 2, 2) + y0
        s_ref[...] = s * wcl + _bdot(u, bt * wcl, 1, 1) + _bdot(vv, kt * wcl, 1, 1)
        return carry

    lax.fori_loop(0, nchunk, chunk, 0)

    y = jnp.concatenate([y3_ref[h] for h in range(nh)], axis=-1)
    yc = y - headsum(y) * (1.0 / hd)
    var = headsum(yc * yc) * (1.0 / hd)
    yn = yc * lax.rsqrt(var + RW_LNX_EPS)
    bonus = headsum(rk2) * v
    y_ref[...] = ((yn * lng_ref[...] + lnb_ref[...] + bonus) * g).astype(y_ref.dtype)


def _rwkv_call(z_rw, mu, w0, w_up, a0, a_up, g_up, k_k, k_a, r_k, lnx_g, lnx_b):
    t = z_rw.shape[0]
    tb = min(RW_BLOCK, t)
    c = RW_CHUNK
    tri = jnp.asarray(np.kron(np.eye(tb // c), np.tril(np.ones((c, c)))), BF16)
    bd = jnp.asarray(np.kron(np.eye(256 // RW_HEAD), np.ones((RW_HEAD, RW_HEAD))), BF16)
    row = lambda n: pl.BlockSpec((1, n), lambda i: (0, 0))
    full = lambda a: pl.BlockSpec(a.shape, lambda i: (0,) * a.ndim)
    st = lambda: pltpu.VMEM((RW_HEADS, tb, RW_HEAD), F32)
    return pl.pallas_call(
        _rwkv_kernel,
        grid=(t // tb,),
        in_specs=[
            pl.BlockSpec((tb, RW_COLS_PAD), lambda i: (i, 0)),
            pl.BlockSpec((8, RW_COLS_PAD), lambda i: (jnp.maximum(i * (tb // 8) - 1, 0), 0)),
            row(RW_COLS_PAD), row(MIX_W), full(w_up), row(MIX_W), full(a_up), full(g_up),
            row(MIX_W), row(MIX_W), row(MIX_W), row(MIX_W), row(MIX_W), full(tri), full(bd),
        ],
        out_specs=pl.BlockSpec((tb, MIX_W), lambda i: (i, 0)),
        out_shape=jax.ShapeDtypeStruct((t, MIX_W), BF16),
        scratch_shapes=[pltpu.VMEM((RW_HEADS, RW_HEAD, RW_HEAD), F32),
                        st(), st(), st(), st(), st(), st(), st()],
        compiler_params=_cparams("arbitrary"),
        name="rwkv7",
    )(z_rw, z_rw, mu, w0, w_up, a0, a_up, g_up, k_k, k_a, r_k, lnx_g, lnx_b, tri, bd)


def _proj_tok_kernel(h_ref, g_ref, w_ref, o_ref, xn_ref):
    @pl.when(pl.program_id(1) == 0)
    def _():
        xn_ref[...] = _rms(h_ref[...], g_ref[...]).astype(BF16)

    o_ref[...] = jnp.dot(xn_ref[...], w_ref[...], preferred_element_type=F32).astype(o_ref.dtype)


def _proj_tok(h, g, w, layer, tn, out_dtype=F32, tm=1024):
    t, d = h.shape
    n = w.shape[2]
    tm = min(tm, t)
    return pl.pallas_call(
        _proj_tok_kernel,
        grid=(t // tm, n // tn),
        in_specs=[pl.BlockSpec((tm, d), lambda i, j: (i, 0)),
                  pl.BlockSpec((1, d), lambda i, j: (0, 0)),
                  pl.BlockSpec((None, d, tn), lambda i, j: (layer, 0, j))],
        out_specs=pl.BlockSpec((tm, tn), lambda i, j: (i, j)),
        out_shape=jax.ShapeDtypeStruct((t, n), out_dtype),
        scratch_shapes=[pltpu.VMEM((tm, d), BF16)],
        compiler_params=_cparams("parallel", "arbitrary"),
        name="proj_tok",
    )(h, g, w)


def _proj_feat_kernel(h_ref, g_ref, wt_ref, o_ref, xn_ref):
    @pl.when(pl.program_id(1) == 0)
    def _():
        xn_ref[...] = _rms(h_ref[...], g_ref[...]).astype(BF16)

    o_ref[...] = lax.dot_general(wt_ref[...], xn_ref[...], (((1,), (1,)), ((), ())),
                                 preferred_element_type=F32)


def _proj_feat(h, g, wt, layer, tn, tm=1024):
    t, d = h.shape
    n = wt.shape[1]
    tm = min(tm, t)
    return pl.pallas_call(
        _proj_feat_kernel,
        grid=(t // tm, n // tn),
        in_specs=[pl.BlockSpec((tm, d), lambda i, j: (i, 0)),
                  pl.BlockSpec((1, d), lambda i, j: (0, 0)),
                  pl.BlockSpec((None, tn, d), lambda i, j: (layer, j, 0))],
        out_specs=pl.BlockSpec((tn, tm), lambda i, j: (j, i)),
        out_shape=jax.ShapeDtypeStruct((n, t), F32),
        scratch_shapes=[pltpu.VMEM((tm, d), BF16)],
        compiler_params=_cparams("parallel", "arbitrary"),
        name="proj_feat",
    )(h, g, wt)


def _ffn_kernel(h_ref, gpre_ref, gpost_ref, wg_ref, wu_ref, wd_ref, o_ref, xn_ref, acc_ref):
    j = pl.program_id(1)

    @pl.when(j == 0)
    def _():
        xn_ref[...] = _rms(h_ref[...], gpre_ref[...]).astype(BF16)
        acc_ref[...] = jnp.zeros_like(acc_ref)

    xn = xn_ref[...]
    gate = jnp.dot(xn, wg_ref[...], preferred_element_type=F32)
    up = jnp.dot(xn, wu_ref[...], preferred_element_type=F32)
    act = (gate * _sigmoid(gate) * up).astype(BF16)
    acc_ref[...] += jnp.dot(act, wd_ref[...], preferred_element_type=F32)

    @pl.when(j == pl.num_programs(1) - 1)
    def _():
        o_ref[...] = h_ref[...] + HALF * _rms(acc_ref[...], gpost_ref[...])


def _ffn(h, gpre, gpost, wg, wu, wd, layer, tm=512, tf=512):
    t, d = h.shape
    f = wg.shape[2]
    tm = min(tm, t)
    return pl.pallas_call(
        _ffn_kernel,
        grid=(t // tm, f // tf),
        in_specs=[pl.BlockSpec((tm, d), lambda i, j: (i, 0)),
                  pl.BlockSpec((1, d), lambda i, j: (0, 0)),
                  pl.BlockSpec((1, d), lambda i, j: (0, 0)),
                  pl.BlockSpec((None, d, tf), lambda i, j: (layer, 0, j)),
                  pl.BlockSpec((None, d, tf), lambda i, j: (layer, 0, j)),
                  pl.BlockSpec((None, tf, d), lambda i, j: (layer, j, 0))],
        out_specs=pl.BlockSpec((tm, d), lambda i, j: (i, 0)),
        out_shape=jax.ShapeDtypeStruct((t, d), F32),
        scratch_shapes=[pltpu.VMEM((tm, d), BF16), pltpu.VMEM((tm, d), F32)],
        compiler_params=_cparams("parallel", "arbitrary"),
        name="ffn",
    )(h, gpre, gpost, wg, wu, wd)


def _branch_kernel(yr_ref, yd_ref, ym_ref, gr_ref, gd_ref, gm_ref, wb_ref, o_ref):
    acc = _sigmoid(gr_ref[...].astype(F32)) * jnp.dot(yr_ref[...], wb_ref[0], preferred_element_type=F32)
    acc += _sigmoid(gd_ref[...].astype(F32)) * jnp.dot(yd_ref[...], wb_ref[1], preferred_element_type=F32)
    acc += _sigmoid(gm_ref[...].astype(F32)) * jnp.dot(ym_ref[...], wb_ref[2], preferred_element_type=F32)
    o_ref[...] = acc.astype(o_ref.dtype)


def _branch_merge(y_rw, y_dsa, y_ml, z_gate, wb, layer, tm=1024, tn=512):
    t = y_rw.shape[0]
    d = D_MODEL
    tm = min(tm, t)
    nj = d // tn
    yspec = pl.BlockSpec((tm, MIX_W), lambda i, j: (i, 0))
    gspec = lambda b: pl.BlockSpec((tm, tn), lambda i, j: (i, b * nj + j))
    return pl.pallas_call(
        _branch_kernel,
        grid=(t // tm, nj),
        in_specs=[yspec, yspec, yspec, gspec(0), gspec(1), gspec(2),
                  pl.BlockSpec((None, 3, MIX_W, tn), lambda i, j: (layer, 0, 0, j))],
        out_specs=pl.BlockSpec((tm, tn), lambda i, j: (i, j)),
        out_shape=jax.ShapeDtypeStruct((t, d), BF16),
        compiler_params=_cparams("parallel", "arbitrary"),
        name="branch_merge",
    )(y_rw, y_dsa, y_ml, z_gate, z_gate, z_gate, wb)


def _outproj_kernel(h_ref, x_ref, g_ref, w_ref, o_ref):
    mix = jnp.dot(x_ref[...], w_ref[...], preferred_element_type=F32)
    o_ref[...] = h_ref[...] + _rms(mix, g_ref[...])


def _outproj(h, x, g, w, layer, tm=512):
    t, d = h.shape
    tm = min(tm, t)
    return pl.pallas_call(
        _outproj_kernel,
        grid=(t // tm,),
        in_specs=[pl.BlockSpec((tm, d), lambda i: (i, 0)),
                  pl.BlockSpec((tm, d), lambda i: (i, 0)),
                  pl.BlockSpec((1, d), lambda i: (0, 0)),
                  pl.BlockSpec((None, d, d), lambda i: (layer, 0, 0))],
        out_specs=pl.BlockSpec((tm, d), lambda i: (i, 0)),
        out_shape=jax.ShapeDtypeStruct((t, d), F32),
        compiler_params=_cparams("parallel"),
        name="outproj",
    )(h, x, g, w)


def _ple_kernel(h_ref, p_ref, g6_ref, g7_ref, wg_ref, wp_ref, o_ref):
    h = h_ref[...]
    pg = _sigmoid(jnp.dot(_rms(h, g6_ref[...]).astype(BF16), wg_ref[...], preferred_element_type=F32))
    e = jnp.dot(p_ref[...].astype(BF16), wp_ref[...], preferred_element_type=F32)
    o_ref[...] = h + _rms(pg * e, g7_ref[...])


def _ple(h, p, g6, g7, wg, wp, layer, tm=512):
    t, d = h.shape
    tm = min(tm, t)
    return pl.pallas_call(
        _ple_kernel,
        grid=(t // tm,),
        in_specs=[pl.BlockSpec((tm, d), lambda i: (i, 0)),
                  pl.BlockSpec((None, tm, PLE_DIM), lambda i: (layer, i, 0)),
                  pl.BlockSpec((1, d), lambda i: (0, 0)),
                  pl.BlockSpec((1, d), lambda i: (0, 0)),
                  pl.BlockSpec((None, d, d), lambda i: (layer, 0, 0)),
                  pl.BlockSpec((None, PLE_DIM, d), lambda i: (layer, 0, 0))],
        out_specs=pl.BlockSpec((tm, d), lambda i: (i, 0)),
        out_shape=jax.ShapeDtypeStruct((t, d), F32),
        compiler_params=_cparams("parallel"),
        name="ple",
    )(h, p, g6, g7, wg, wp)


def _log_sigmoid(x):
    return jnp.minimum(x, 0.0) - jnp.log(1.0 + jnp.exp(-jnp.abs(x)))


def _mlstm_kernel(z_ref, gt_ref, conv_ref, bcol_ref, brow_ref, hng_ref, tri_ref, y_ref,
                  c_ref, n_ref, m_ref, prev_ref):
    i = pl.program_id(0)
    ln = z_ref.shape[0]
    nh, dk, dv = ML_HEADS, ML_QK, ML_V
    nqk = 2 * nh * dk

    @pl.when(i == 0)
    def _():
        c_ref[...] = jnp.zeros_like(c_ref)
        n_ref[...] = jnp.zeros_like(n_ref)
        m_ref[...] = jnp.zeros_like(m_ref)
        prev_ref[...] = jnp.zeros_like(prev_ref)

    x = z_ref[:, 0:nqk]
    prev = prev_ref[...]
    row8 = lax.broadcasted_iota(jnp.int32, (8, 1), 0)
    w = conv_ref[...]
    acc = x * w[ML_CONV - 1:ML_CONV, :]
    for d in range(1, ML_CONV):
        sh = pltpu.roll(x, d, 0)
        top = jnp.where(row8 < d, pltpu.roll(prev, d, 0), sh[0:8, :])
        sh = jnp.concatenate([top, sh[8:, :]], axis=0)
        acc = acc + sh * w[ML_CONV - 1 - d:ML_CONV - d, :]
    prev_ref[...] = x[ln - 8:ln, :]
    qk = acc * _sigmoid(acc)
    q = qk[:, 0:nh * dk] * (dk ** -0.5)
    k = qk[:, nh * dk:nqk]

    gcol = z_ref[:, nqk + 2 * MIX_W:nqk + 2 * MIX_W + 2 * nh] + bcol_ref[...]
    grow = gt_ref[...] + brow_ref[...]
    lcol = _log_sigmoid(gcol)
    lrow = _log_sigmoid(grow)
    tri = tri_ref[...]
    c3 = _split3(lcol)
    cum_col = sum(jnp.dot(tri, t3, preferred_element_type=F32) for t3 in c3)
    r3 = _split3(lrow)
    cum_row = sum(lax.dot_general(t3, tri, (((1,), (1,)), ((), ())), preferred_element_type=F32) for t3 in r3)

    ti = lax.broadcasted_iota(jnp.int32, (ln, ln), 0)
    si = lax.broadcasted_iota(jnp.int32, (ln, ln), 1)
    tril = si <= ti
    for h in range(nh):
        g_col = cum_col[:, nh + h:nh + h + 1]
        g_row = cum_row[nh + h:nh + h + 1, :]
        i_col = gcol[:, h:h + 1]
        i_row = grow[h:h + 1, :]
        m_st = m_ref[h, 0:1, 0:1]
        c_st = c_ref[h]
        n_st = n_ref[h, 0:1, :]
        qh = q[:, h * dk:(h + 1) * dk]
        kh = k[:, h * dk:(h + 1) * dk]
        vh = z_ref[:, nqk + h * dv:nqk + (h + 1) * dv]
        oh = z_ref[:, nqk + MIX_W + h * dv:nqk + MIX_W + (h + 1) * dv]
        a_log = g_col + m_st
        d_log = jnp.where(tril, g_col - g_row + i_row, -jnp.inf)
        m_q = jnp.maximum(a_log, jnp.max(d_log, axis=-1, keepdims=True))
        inter = jnp.exp(a_log - m_q)
        s_w = lax.dot_general(qh.astype(BF16), kh.astype(BF16), (((1,), (1,)), ((), ())),
                              preferred_element_type=F32) * jnp.exp(d_log - m_q)
        num = inter * _dot(qh, c_st) + _dot(s_w, vh)
        den = inter * jnp.sum(qh * n_st, axis=-1, keepdims=True) + jnp.sum(s_w, axis=-1, keepdims=True)
        hh = num / jnp.maximum(jnp.abs(den), jnp.exp(-m_q))
        g_tot = g_col[ln - 1:ln, :]
        w_log = g_tot - g_col + i_col
        m_new = jnp.maximum(g_tot + m_st, jnp.max(w_log, axis=0, keepdims=True))
        decay = jnp.exp(g_tot + m_st - m_new)
        kw = kh * jnp.exp(w_log - m_new)
        c_ref[h] = decay * c_st + lax.dot_general(kw.astype(BF16), vh.astype(BF16), (((0,), (0,)), ((), ())),
                                                  preferred_element_type=F32)
        n_ref[h] = jnp.broadcast_to(decay * n_st + jnp.sum(kw, axis=0, keepdims=True), (8, dk))
        m_ref[h] = jnp.broadcast_to(m_new, (8, 128))
        hn = hh * lax.rsqrt(jnp.mean(hh * hh, axis=-1, keepdims=True) + EPS) * hng_ref[:, h * dv:(h + 1) * dv]
        y_ref[:, h * dv:(h + 1) * dv] = (_sigmoid(oh) * hn).astype(y_ref.dtype)


def _mlstm_call(z_ml, zt, gate_row0, conv_w, b_col, b_row, hn_g):
    t = z_ml.shape[0]
    ln = ML_CHUNK
    tri = jnp.asarray(np.tril(np.ones((ln, ln))), BF16)
    full = lambda a: pl.BlockSpec(a.shape, lambda i: (0,) * a.ndim)
    return pl.pallas_call(
        _mlstm_kernel,
        grid=(t // ln,),
        in_specs=[pl.BlockSpec((ln, ML_COLS_PAD), lambda i: (i, 0)),
                  pl.BlockSpec((8, ln), lambda i: (gate_row0 // 8, i)),
                  full(conv_w), full(b_col), full(b_row), full(hn_g), full(tri)],
        out_specs=pl.BlockSpec((ln, MIX_W), lambda i: (i, 0)),
        out_shape=jax.ShapeDtypeStruct((t, MIX_W), BF16),
        scratch_shapes=[pltpu.VMEM((ML_HEADS, ML_QK, ML_V), F32),
                        pltpu.VMEM((ML_HEADS, 8, ML_QK), F32),
                        pltpu.VMEM((ML_HEADS, 8, 128), F32),
                        pltpu.VMEM((8, 2 * ML_HEADS * ML_QK), F32)],
        compiler_params=_cparams("arbitrary"),
        name="mlstm",
    )(z_ml, zt, conv_w, b_col, b_row, hn_g, tri)


DSA_TQ = 256
DSA_GROUP = 4
INT_MIN = -2 ** 31
NEG = -1e30
LOG2E = math.log2(math.e)
SHIFT_SLACK = 8.0


def _t5_bucket_bounds():
    max_exact = REL_BUCKETS // 2
    d = np.arange(1, 4 * REL_MAX_DIST, dtype=np.float64)
    large = max_exact + np.floor(np.log(d / max_exact) / math.log(REL_MAX_DIST / max_exact)
                                 * (REL_BUCKETS - max_exact)).astype(np.int64)
    bucket = np.where(d < max_exact, d.astype(np.int64), np.minimum(large, REL_BUCKETS - 1))
    return [int(d[np.argmax(bucket >= b)]) for b in range(max_exact + 1, REL_BUCKETS)]


def _bias_tile_kernel(rb_ref, o_ref):
    o = pl.program_id(0)
    h = pl.program_id(1)
    tq = o_ref.shape[-1]
    s_loc = lax.broadcasted_iota(jnp.int32, (tq, tq), 0)
    t_loc = lax.broadcasted_iota(jnp.int32, (tq, tq), 1)
    dist = o * tq + t_loc - s_loc
    max_exact = REL_BUCKETS // 2
    bucket = jnp.full((tq, tq), max_exact, jnp.int32)
    for bnd in _t5_bucket_bounds():
        bucket = bucket + (dist >= bnd).astype(jnp.int32)
    bucket = jnp.where(dist < max_exact, jnp.maximum(dist, 0), bucket)
    bias = jnp.zeros((tq, tq), F32)
    for b in range(REL_BUCKETS):
        bias = jnp.where(bucket == b, rb_ref[b, h], bias)
    o_ref[...] = (bias - rb_ref[REL_BUCKETS - 1, h]) * LOG2E


def _bias_tiles(rel_bias, tq):
    assert tq >= REL_MAX_DIST
    return pl.pallas_call(
        _bias_tile_kernel,
        grid=(3, DSA_HEADS),
        in_specs=[pl.BlockSpec(memory_space=pltpu.SMEM)],
        out_specs=pl.BlockSpec((None, None, tq, tq), lambda o, h: (o, h, 0, 0)),
        out_shape=jax.ShapeDtypeStruct((3, DSA_HEADS, tq, tq), F32),
        name="t5_bias_tiles",
    )(rel_bias)


def _dsa_prep_kernel(zt_ref, ckvt_ref, g_ref, gcol_ref, gk_ref, ck_ref, ki_ref, ct_ref):
    ckv = zt_ref[:, 0:DSA_KV_RANK]
    ck_ref[...] = _rms(ckv, g_ref[...]).astype(BF16)
    ki_ref[...] = _rms(zt_ref[:, DSA_KV_RANK:DSA_KV_RANK + IDX_DIM], gk_ref[...]).astype(BF16)
    ct = ckvt_ref[...]
    ms = jnp.mean(ct * ct, axis=0, keepdims=True)
    ct_ref[...] = (ct * lax.rsqrt(ms + EPS) * gcol_ref[...]).astype(BF16)


def _dsa_prep(z_dtok, zt, ckv_row0, g_row, g_col, gk_row):
    t = z_dtok.shape[0]
    tq = DSA_TQ
    full = lambda a: pl.BlockSpec(a.shape, lambda i: (0,) * a.ndim)
    return pl.pallas_call(
        _dsa_prep_kernel,
        grid=(t // tq,),
        in_specs=[pl.BlockSpec((tq, z_dtok.shape[1]), lambda i: (i, 0)),
                  pl.BlockSpec((DSA_KV_RANK, tq), lambda i: (ckv_row0 // DSA_KV_RANK, i)),
                  full(g_row), full(g_col), full(gk_row)],
        out_specs=[pl.BlockSpec((tq, DSA_KV_RANK), lambda i: (i, 0)),
                   pl.BlockSpec((tq, IDX_DIM), lambda i: (i, 0)),
                   pl.BlockSpec((None, DSA_KV_RANK, tq), lambda i: (i, 0, 0))],
        out_shape=[jax.ShapeDtypeStruct((t, DSA_KV_RANK), BF16),
                   jax.ShapeDtypeStruct((t, IDX_DIM), BF16),
                   jax.ShapeDtypeStruct((t // tq, DSA_KV_RANK, tq), BF16)],
        compiler_params=_cparams("parallel"),
        name="dsa_prep",
    )(z_dtok, zt, g_row, g_col, gk_row)


def _dsa_kernel(topk, qt_ref, qit_ref, wt_ref, ki_ref, ck_ref, ct_ref, wuk_ref, wuv_ref, bias_ref,
                o_ref, key_ref, acc_ref, m_ref, l_ref, qlat_ref, qi_ref, s_ref, tmax_ref):
    i = pl.program_id(0)
    tq = DSA_TQ
    nkb = i + 1
    nh, hd, r = DSA_HEADS, DSA_HEAD_DIM, DSA_KV_RANK

    for h in range(nh):
        ql = jnp.dot(wuk_ref[h], qt_ref[h * hd:(h + 1) * hd, :].astype(BF16), preferred_element_type=F32)
        qlat_ref[h] = (ql * (hd ** -0.5 * LOG2E)).astype(BF16)
    qi_ref[...] = qit_ref[...].astype(BF16)
    wrow = wt_ref[...] * ((IDX_DIM ** -0.5) * (IDX_HEADS ** -0.5))
    s_loc = lax.broadcasted_iota(jnp.int32, (tq, tq), 0)
    t_loc = lax.broadcasted_iota(jnp.int32, (tq, tq), 1)

    def score_blk(kb, carry):
        kblk = ki_ref[pl.ds(pl.multiple_of(kb * tq, tq), tq), :]
        acc = jnp.zeros((tq, tq), F32)
        for h in range(IDX_HEADS):
            rel = jnp.dot(kblk, qi_ref[h * IDX_DIM:(h + 1) * IDX_DIM, :], preferred_element_type=F32)
            acc = acc + wrow[h:h + 1, :] * jnp.maximum(rel, 0.0)
        bits = pltpu.bitcast(acc, jnp.int32)
        key = jnp.where(bits < 0, bits ^ 0x7FFFFFFF, bits)
        causal = (kb * tq + s_loc) <= (i * tq + t_loc)
        key_ref[kb] = jnp.where(causal, key, INT_MIN)
        return carry

    lax.fori_loop(0, nkb, score_blk, 0)

    grp = DSA_GROUP
    ngrp = (i + grp) // grp

    def pad_blk(kb, carry):
        key_ref[kb] = jnp.full((tq, tq), INT_MIN, jnp.int32)
        return carry

    lax.fori_loop(nkb, ngrp * grp, pad_blk, 0)

    def count_ge(cand):
        def body(g, cnt):
            for j in range(grp):
                ge = (key_ref[g * grp + j] >= cand).astype(jnp.int32)
                cnt = cnt + jnp.sum(ge.reshape(tq // 8, 8, tq), axis=0)
            return cnt
        cnt = lax.fori_loop(0, ngrp, body, jnp.zeros((8, tq), jnp.int32))
        return jnp.sum(cnt, axis=0, keepdims=True)

    few = (i * tq + t_loc[0:1, :]) < topk

    def search_cond(c):
        return jnp.logical_and(c[0] < 32, c[3] > 0)

    def search_body(c):
        step, cur, cnt, _ = c
        cand = cur + jnp.left_shift(jnp.int32(1), 31 - step)
        cnt_c = count_ge(cand)
        take = cnt_c >= topk
        cur = jnp.where(take, cand, cur)
        cnt = jnp.where(take, cnt_c, cnt)
        pending = jnp.sum(jnp.where(few, 0, (cnt != topk).astype(jnp.int32)))
        return step + 1, cur, cnt, pending

    init = (jnp.int32(0), jnp.full((1, tq), INT_MIN, jnp.int32), jnp.full((1, tq), nkb * tq, jnp.int32),
            jnp.int32(1))
    thr = lax.while_loop(search_cond, search_body, init)[1]
    thr = jnp.where(few, INT_MIN + 1, jnp.maximum(thr, INT_MIN + 1))

    m_ref[...] = jnp.full_like(m_ref, NEG)
    l_ref[...] = jnp.zeros_like(l_ref)
    acc_ref[...] = jnp.zeros_like(acc_ref)

    def att_grp(near, g, carry):
        excess = jnp.full((1, tq), NEG, F32)
        tmax = [None] * nh
        for j in range(grp):
            kb = g * grp + j
            addm = jnp.where(key_ref[kb] >= thr, 0.0, NEG)
            cblk = ck_ref[pl.ds(pl.multiple_of(kb * tq, tq), tq), :]
            for h in range(nh):
                s = jnp.dot(cblk, qlat_ref[h], preferred_element_type=F32) + addm
                if near:
                    s = s + bias_ref[jnp.clip(i - kb, 0, 2), h]
                s_ref[j * nh + h] = s
                tm = jnp.max(s, axis=0, keepdims=True)
                tmax[h] = tm if j == 0 else jnp.maximum(tmax[h], tm)
        for h in range(nh):
            tmax_ref[h] = tmax[h]
            excess = jnp.maximum(excess, tmax[h] - m_ref[h])

        @pl.when(jnp.max(excess) > SHIFT_SLACK)
        def _():
            for h in range(nh):
                m_old = m_ref[h]
                m_new = jnp.maximum(m_old, tmax_ref[h])
                corr = jnp.exp2(m_old - m_new)
                l_ref[h] = l_ref[h] * corr
                acc_ref[h] = acc_ref[h] * corr
                m_ref[h] = m_new

        ctg = jnp.concatenate([ct_ref[g * grp + j] for j in range(grp)], axis=1)
        for h in range(nh):
            m_h = m_ref[h]
            p = jnp.concatenate([jnp.exp2(s_ref[j * nh + h] - m_h) for j in range(grp)], axis=0)
            l_ref[h] = l_ref[h] + jnp.sum(p, axis=0, keepdims=True)
            acc_ref[h] = acc_ref[h] + jnp.dot(ctg, p.astype(BF16), preferred_element_type=F32)
        return carry

    nfar = jnp.maximum(i - 1, 0) // grp
    lax.fori_loop(0, nfar, functools.partial(att_grp, False), 0)
    lax.fori_loop(nfar, ngrp, functools.partial(att_grp, True), 0)

    for h in range(nh):
        o_lat = (acc_ref[h] / l_ref[h]).astype(BF16)
        out_t = jnp.dot(wuv_ref[h], o_lat, preferred_element_type=F32)
        o_ref[:, h * hd:(h + 1) * hd] = out_t.T.astype(o_ref.dtype)


def _dsa_call(zt, q_row0, qi_row0, w_row0, ki, ck, ct, wuk_t, wuv_t, bias_tiles, topk):
    t = zt.shape[1]
    tq = DSA_TQ
    nq = DSA_HEADS * DSA_HEAD_DIM
    ni = IDX_HEADS * IDX_DIM
    full = lambda a: pl.BlockSpec(a.shape, lambda i: (0,) * a.ndim, pipeline_mode=pl.Buffered(1))
    return pl.pallas_call(
        functools.partial(_dsa_kernel, topk),
        grid=(t // tq,),
        in_specs=[pl.BlockSpec((nq, tq), lambda i: (q_row0 // nq, i)),
                  pl.BlockSpec((ni, tq), lambda i: (qi_row0 // ni, i)),
                  pl.BlockSpec((IDX_HEADS, tq), lambda i: (w_row0 // IDX_HEADS, i)),
                  full(ki), full(ck), full(ct), full(wuk_t), full(wuv_t), full(bias_tiles)],
        out_specs=pl.BlockSpec((tq, MIX_W), lambda i: (i, 0)),
        out_shape=jax.ShapeDtypeStruct((t, MIX_W), BF16),
        scratch_shapes=[pltpu.VMEM((t // tq, tq, tq), jnp.int32),
                        pltpu.VMEM((DSA_HEADS, DSA_KV_RANK, tq), F32),
                        pltpu.VMEM((DSA_HEADS, 1, tq), F32),
                        pltpu.VMEM((DSA_HEADS, 1, tq), F32),
                        pltpu.VMEM((DSA_HEADS, DSA_KV_RANK, tq), BF16),
                        pltpu.VMEM((ni, tq), BF16),
                        pltpu.VMEM((DSA_GROUP * DSA_HEADS, tq, tq), F32),
                        pltpu.VMEM((DSA_HEADS, 1, tq), F32)],
        compiler_params=_cparams("arbitrary"),
        name="dsa",
    )(zt, zt, zt, ki, ck, ct, wuk_t, wuv_t, bias_tiles)


FEAT_Q0 = 0
FEAT_QI0 = FEAT_Q0 + DSA_HEADS * DSA_HEAD_DIM
FEAT_CKV0 = FEAT_QI0 + IDX_HEADS * IDX_DIM
FEAT_W0 = FEAT_CKV0 + DSA_KV_RANK
FEAT_MLG0 = FEAT_W0 + IDX_HEADS
FEAT_ROWS = 2560
DTOK_COLS = 384


def _pad_last(a, n):
    return jnp.pad(a, [(0, 0)] * (a.ndim - 1) + [(0, n - a.shape[-1])])


def kernel(x, p, norm_gains, ffn1_gate, ffn1_up, ffn1_down, w_in, rw_mu, rw_w0, rw_w_up, rw_a0, rw_a_up, rw_g_up, rw_k_k, rw_k_a, rw_r_k, rw_lnx_g, rw_lnx_b, dsa_ckv_g, dsa_kidx_g, dsa_w_uk, dsa_w_uv, rel_bias, ml_conv, ml_b_i, ml_b_f, ml_hn_g, w_branch, w_out, ffn2_gate, ffn2_up, ffn2_down, ple_proj, ple_gate):
    b, t, d = x.shape
    assert b == 1 and d == D_MODEL
    nl = w_in.shape[0]
    h = x.reshape(t, d)
    p = p.reshape(nl, t, PLE_DIM)
    bf = lambda a: a.astype(BF16)

    f1g, f1u, f1d = bf(ffn1_gate), bf(ffn1_up), bf(ffn1_down)
    f2g, f2u, f2d = bf(ffn2_gate), bf(ffn2_up), bf(ffn2_down)
    wb, wo, wpg, wpp = bf(w_branch), bf(w_out), bf(ple_gate), bf(ple_proj)
    o_dsa = RW_COLS
    o_ml = o_dsa + DSA_COLS
    o_gate = o_ml + ML_COLS
    w_rw = bf(_pad_last(w_in[:, :, 0:RW_COLS], RW_COLS_PAD))
    w_ml = bf(_pad_last(w_in[:, :, o_ml:o_gate], ML_COLS_PAD))
    w_gate = bf(w_in[:, :, o_gate:])
    nq = DSA_HEADS * DSA_HEAD_DIM
    ni = IDX_HEADS * IDX_DIM
    w_q = w_in[:, :, o_dsa:o_dsa + nq]
    w_ckv = w_in[:, :, o_dsa + nq:o_dsa + nq + DSA_KV_RANK]
    w_qi = w_in[:, :, o_dsa + nq + DSA_KV_RANK:o_dsa + nq + DSA_KV_RANK + ni]
    w_ki = w_in[:, :, o_dsa + nq + DSA_KV_RANK + ni:o_dsa + nq + DSA_KV_RANK + ni + IDX_DIM]
    w_wi = w_in[:, :, o_dsa + DSA_COLS - IDX_HEADS:o_dsa + DSA_COLS]
    w_mlg = w_in[:, :, o_gate - 2 * ML_HEADS:o_gate]
    w_dtok = bf(_pad_last(jnp.concatenate([w_ckv, w_ki], axis=-1), DTOK_COLS))
    w_feat = bf(jnp.swapaxes(_pad_last(jnp.concatenate([w_q, w_qi, w_ckv, w_wi, w_mlg], axis=-1), FEAT_ROWS), 1, 2))
    mu = _pad_last(rw_mu, RW_COLS_PAD)
    wup, aup = bf(rw_w_up), bf(rw_a_up)
    gup = bf(jnp.pad(rw_g_up, ((0, 0), (0, RW_GATE_PAD - RW_GATE_LORA), (0, 0))))
    wuk_t = bf(jnp.swapaxes(dsa_w_uk, 2, 3))
    wuv_t = bf(jnp.swapaxes(dsa_w_uv, 2, 3))
    ml_b = jnp.concatenate([ml_b_i, ml_b_f], axis=-1)

    bias_tiles = _bias_tiles(rel_bias, DSA_TQ)
    topk = min(TOPK_MAX, t // 4)
    row = lambda a: a.reshape(1, -1)

    for i in range(nl):
        g = lambda k: norm_gains[i, k].reshape(1, d)
        h = _ffn(h, g(0), g(1), f1g, f1u, f1d, i)

        z_rw = _proj_tok(h, g(2), w_rw, i, tn=1152)
        z_ml = _proj_tok(h, g(2), w_ml, i, tn=640)
        z_gate = _proj_tok(h, g(2), w_gate, i, tn=1024, out_dtype=BF16)
        z_dtok = _proj_tok(h, g(2), w_dtok, i, tn=DTOK_COLS)
        zt = _proj_feat(h, g(2), w_feat, i, tn=1280)

        y_rw = _rwkv_call(z_rw, row(mu[i]), row(rw_w0[i]), wup[i], row(rw_a0[i]), aup[i], gup[i],
                          row(rw_k_k[i]), row(rw_k_a[i]), row(rw_r_k[i]), row(rw_lnx_g[i]), row(rw_lnx_b[i]))
        ck, ki, ct = _dsa_prep(z_dtok, zt, FEAT_CKV0, row(dsa_ckv_g[i]), dsa_ckv_g[i].reshape(-1, 1),
                               row(dsa_kidx_g[i]))
        y_dsa = _dsa_call(zt, FEAT_Q0, FEAT_QI0, FEAT_W0, ki, ck, ct, wuk_t[i], wuv_t[i], bias_tiles, topk)
        y_ml = _mlstm_call(z_ml, zt, FEAT_MLG0, ml_conv[i], row(ml_b[i]), ml_b[i].reshape(-1, 1), row(ml_hn_g[i]))

        mixsum = _branch_merge(y_rw, y_dsa, y_ml, z_gate, wb, i)
        h = _outproj(h, mixsum, g(3), wo, i)
        h = _ffn(h, g(4), g(5), f2g, f2u, f2d, i)
        h = _ple(h, p, g(6), g(7), wpg, wpp, i)
    return h.reshape(b, t, d)
```

```python
import functools
import math

import jax
import jax.numpy as jnp
import numpy as np
from jax import lax
from jax.experimental import pallas as pl
from jax.experimental.pallas import tpu as pltpu

F32 = jnp.float32
BF16 = jnp.bfloat16

D_MODEL = 2048
DEPTH = 4
PLE_DIM = 256
D_FF = 5632
MIX_W = 1024
EPS = 1e-6
HALF = 0.5

RW_HEAD = 64
RW_HEADS = 16
RW_DECAY_LORA = 64
RW_AAA_LORA = 64
RW_GATE_LORA = 160
RW_LNX_EPS = 64e-5
RW_COLS = 3 * MIX_W + RW_DECAY_LORA + RW_AAA_LORA + RW_GATE_LORA
RW_COLS_PAD = 3456
RW_GATE_PAD = RW_COLS_PAD - (3 * MIX_W + RW_DECAY_LORA + RW_AAA_LORA)

DSA_HEAD_DIM = 128
DSA_HEADS = 8
DSA_KV_RANK = 256
IDX_HEADS = 16
IDX_DIM = 64
TOPK_MAX = 256
DSA_COLS = DSA_HEADS * DSA_HEAD_DIM + DSA_KV_RANK + IDX_HEADS * IDX_DIM + IDX_DIM + IDX_HEADS

ML_HEADS = 4
ML_V = 256
ML_QK = 128
ML_CONV = 4
ML_CHUNK = 128
ML_COLS = 2 * ML_HEADS * ML_QK + 2 * MIX_W + 2 * ML_HEADS
ML_COLS_PAD = 3200

REL_BUCKETS = 32
REL_MAX_DIST = 128

VMEM_LIMIT_BYTES = 56 * 1024 * 1024

RW_CHUNK = 64
RW_BLOCK = 256


def _cparams(*sem):
    return pltpu.CompilerParams(dimension_semantics=sem, vmem_limit_bytes=VMEM_LIMIT_BYTES)


def _rms(x, g, eps=EPS):
    ms = jnp.mean(x * x, axis=-1, keepdims=True)
    return x * lax.rsqrt(ms + eps) * g


def _sigmoid(x):
    return 1.0 / (1.0 + jnp.exp(-x))


def _softplus(x):
    return jnp.maximum(x, 0.0) + jnp.log(1.0 + jnp.exp(-jnp.abs(x)))


def _split3(x):
    hi = x.astype(BF16)
    r1 = x - hi.astype(F32)
    mid = r1.astype(BF16)
    lo = (r1 - mid.astype(F32)).astype(BF16)
    return hi, mid, lo


def _dot(a, b):
    return jnp.dot(a.astype(BF16), b.astype(BF16), preferred_element_type=F32)


def _bdot(a, b, ca, cb):
    return lax.dot_general(a.astype(BF16), b.astype(BF16), (((ca,), (cb,)), ((0,), (0,))),
                           preferred_element_type=F32)


def _rwkv_kernel(z_ref, zprev_ref, mu_ref, w0_ref, wup_ref, a0_ref, aup_ref, gup_ref,
                 kk_ref, ka_ref, rk_ref, lng_ref, lnb_ref, tri_ref, bd_ref,
                 y_ref,
                 s_ref, at_ref, rt_ref, bt_ref, kt_ref, v_ref, wc_ref, y3_ref):
    i = pl.program_id(0)
    tb = z_ref.shape[0]
    nh, hd, c = RW_HEADS, RW_HEAD, RW_CHUNK
    nchunk = tb // c

    @pl.when(i == 0)
    def _():
        s_ref[...] = jnp.zeros_like(s_ref)

    z = z_ref[...]
    row = lax.broadcasted_iota(jnp.int32, (tb, 1), 0)
    prev_row = jnp.where(i == 0, 0.0, zprev_ref[7:8, :])
    shifted = jnp.where(row == 0, prev_row, pltpu.roll(z, 1, 0))
    zl = z + (shifted - z) * mu_ref[...]
    m = MIX_W
    r = zl[:, 0:m]
    k = zl[:, m:2 * m]
    v = zl[:, 2 * m:3 * m]
    wd = zl[:, 3 * m:3 * m + 64]
    ad = zl[:, 3 * m + 64:3 * m + 128]
    gd = zl[:, 3 * m + 128:]
    wpre = w0_ref[...] + _dot(jnp.tanh(wd), wup_ref[...])
    lw = -jnp.exp(-_softplus(-wpre) - 0.5)
    a = _sigmoid(a0_ref[...] + _dot(ad, aup_ref[...]))
    g = _dot(_sigmoid(gd), gup_ref[...])
    kkr = k * kk_ref[...]
    k2 = k * (1.0 + (a - 1.0) * ka_ref[...])
    rk2 = r * k2 * rk_ref[...]

    tri = tri_ref[...]
    h3, m3, l3 = _split3(lw)
    cum = (jnp.dot(tri, h3, preferred_element_type=F32) + jnp.dot(tri, m3, preferred_element_type=F32)
           + jnp.dot(tri, l3, preferred_element_type=F32))
    wc = jnp.exp(cum)
    winv = jnp.exp(-cum)
    wprev = jnp.exp(cum - lw)

    def stack(x):
        return jnp.stack([x[:, h * hd:(h + 1) * hd] for h in range(nh)], axis=0)

    bd = bd_ref[...]
    nb = bd.shape[0]

    def headsum(x):
        hi = x.astype(BF16)
        lo = (x - hi.astype(F32)).astype(BF16)
        return jnp.concatenate(
            [jnp.dot(hi[:, j:j + nb], bd, preferred_element_type=F32)
             + jnp.dot(lo[:, j:j + nb], bd, preferred_element_type=F32) for j in range(0, m, nb)], axis=1)

    kk = kkr / jnp.maximum(jnp.sqrt(headsum(kkr * kkr)), 1e-12)
    at_ref[...] = stack(-kk * wprev)
    bt_ref[...] = stack(kk * a * winv)
    kt_ref[...] = stack(k2 * winv)
    wc_ref[...] = stack(wc)
    rt_ref[...] = stack(r * wc)
    v_ref[...] = stack(v)

    ti = lax.broadcasted_iota(jnp.int32, (1, c, c), 1)
    si = lax.broadcasted_iota(jnp.int32, (1, c, c), 2)
    strict = si < ti
    incl = si <= ti
    eye = (si == ti).astype(F32)

    def chunk(ci, carry):
        sl = pl.ds(pl.multiple_of(ci * c, c), c)
        at = at_ref[:, sl, :]
        rt = rt_ref[:, sl, :]
        bt = bt_ref[:, sl, :]
        kt = kt_ref[:, sl, :]
        vv = v_ref[:, sl, :]
        wcl = wc_ref[:, pl.ds(ci * c + (c - 1), 1), :]
        ab = _bdot(at, bt, 2, 2)
        ak = _bdot(at, kt, 2, 2)
        rb = _bdot(rt, bt, 2, 2)
        rkm = _bdot(rt, kt, 2, 2)
        lmat = jnp.where(strict, ab, 0.0)
        aks = jnp.where(strict, ak, 0.0)
        rbt = jnp.where(incl, rb, 0.0)
        rkt = jnp.where(incl, rkm, 0.0)
        tinv = eye + lmat
        lp = lmat
        for _ in range(int(math.log2(c)) - 1):
            lp = _bdot(lp, lp, 2, 1)
            tinv = tinv + _bdot(lp, tinv, 2, 1)
        xv = _bdot(aks, vv, 2, 1)
        p = _bdot(tinv, at, 2, 1)
        qm = _bdot(tinv, xv, 2, 1)
        gm = rt + _bdot(rbt, p, 2, 1)
        y0 = _bdot(rbt, qm, 2, 1) + _bdot(rkt, vv, 2, 1)
        s = s_ref[...]
        u = _bdot(p, s, 2, 2) + qm
        y3_ref[:, sl, :] = _bdot(gm, s, 2, 2) + y0
        s_ref[...] = s * wcl + _bdot(u, bt * wcl, 1, 1) + _bdot(vv, kt * wcl, 1, 1)
        return carry

    lax.fori_loop(0, nchunk, chunk, 0)

    y = jnp.concatenate([y3_ref[h] for h in range(nh)], axis=-1)
    yc = y - headsum(y) * (1.0 / hd)
    var = headsum(yc * yc) * (1.0 / hd)
    yn = yc * lax.rsqrt(var + RW_LNX_EPS)
    bonus = headsum(rk2) * v
    y_ref[...] = ((yn * lng_ref[...] + lnb_ref[...] + bonus) * g).astype(y_ref.dtype)


def _rwkv_call(z_rw, mu, w0, w_up, a0, a_up, g_up, k_k, k_a, r_k, lnx_g, lnx_b):
    t = z_rw.shape[0]
    tb = min(RW_BLOCK, t)
    c = RW_CHUNK
    tri = jnp.asarray(np.kron(np.eye(tb // c), np.tril(np.ones((c, c)))), BF16)
    bd = jnp.asarray(np.kron(np.eye(256 // RW_HEAD), np.ones((RW_HEAD, RW_HEAD))), BF16)
    row = lambda n: pl.BlockSpec((1, n), lambda i: (0, 0))
    full = lambda a: pl.BlockSpec(a.shape, lambda i: (0,) * a.ndim)
    st = lambda: pltpu.VMEM((RW_HEADS, tb, RW_HEAD), F32)
    return pl.pallas_call(
        _rwkv_kernel,
        grid=(t // tb,),
        in_specs=[
            pl.BlockSpec((tb, RW_COLS_PAD), lambda i: (i, 0)),
            pl.BlockSpec((8, RW_COLS_PAD), lambda i: (jnp.maximum(i * (tb // 8) - 1, 0), 0)),
            row(RW_COLS_PAD), row(MIX_W), full(w_up), row(MIX_W), full(a_up), full(g_up),
            row(MIX_W), row(MIX_W), row(MIX_W), row(MIX_W), row(MIX_W), full(tri), full(bd),
        ],
        out_specs=pl.BlockSpec((tb, MIX_W), lambda i: (i, 0)),
        out_shape=jax.ShapeDtypeStruct((t, MIX_W), BF16),
        scratch_shapes=[pltpu.VMEM((RW_HEADS, RW_HEAD, RW_HEAD), F32),
                        st(), st(), st(), st(), st(), st(), st()],
        compiler_params=_cparams("arbitrary"),
        name="rwkv7",
    )(z_rw, z_rw, mu, w0, w_up, a0, a_up, g_up, k_k, k_a, r_k, lnx_g, lnx_b, tri, bd)


def _proj_tok_kernel(h_ref, g_ref, w_ref, o_ref, xn_ref):
    @pl.when(pl.program_id(1) == 0)
    def _():
        xn_ref[...] = _rms(h_ref[...], g_ref[...]).astype(BF16)

    o_ref[...] = jnp.dot(xn_ref[...], w_ref[...], preferred_element_type=F32).astype(o_ref.dtype)


def _proj_tok(h, g, w, layer, tn, out_dtype=F32, tm=1024):
    t, d = h.shape
    n = w.shape[2]
    tm = min(tm, t)
    return pl.pallas_call(
        _proj_tok_kernel,
        grid=(t // tm, n // tn),
        in_specs=[pl.BlockSpec((tm, d), lambda i, j: (i, 0)),
                  pl.BlockSpec((1, d), lambda i, j: (0, 0)),
                  pl.BlockSpec((None, d, tn), lambda i, j: (layer, 0, j))],
        out_specs=pl.BlockSpec((tm, tn), lambda i, j: (i, j)),
        out_shape=jax.ShapeDtypeStruct((t, n), out_dtype),
        scratch_shapes=[pltpu.VMEM((tm, d), BF16)],
        compiler_params=_cparams("parallel", "arbitrary"),
        name="proj_tok",
    )(h, g, w)


def _proj_feat_kernel(h_ref, g_ref, wt_ref, o_ref, xn_ref):
    @pl.when(pl.program_id(1) == 0)
    def _():
        xn_ref[...] = _rms(h_ref[...], g_ref[...]).astype(BF16)

    o_ref[...] = lax.dot_general(wt_ref[...], xn_ref[...], (((1,), (1,)), ((), ())),
                                 preferred_element_type=F32)


def _proj_feat(h, g, wt, layer, tn, tm=1024):
    t, d = h.shape
    n = wt.shape[1]
    tm = min(tm, t)
    return pl.pallas_call(
        _proj_feat_kernel,
        grid=(t // tm, n // tn),
        in_specs=[pl.BlockSpec((tm, d), lambda i, j: (i, 0)),
                  pl.BlockSpec((1, d), lambda i, j: (0, 0)),
                  pl.BlockSpec((None, tn, d), lambda i, j: (layer, j, 0))],
        out_specs=pl.BlockSpec((tn, tm), lambda i, j: (j, i)),
        out_shape=jax.ShapeDtypeStruct((n, t), F32),
        scratch_shapes=[pltpu.VMEM((tm, d), BF16)],
        compiler_params=_cparams("parallel", "arbitrary"),
        name="proj_feat",
    )(h, g, wt)


def _ffn_kernel(h_ref, gpre_ref, gpost_ref, wg_ref, wu_ref, wd_ref, o_ref, xn_ref, acc_ref):
    j = pl.program_id(1)

    @pl.when(j == 0)
    def _():
        xn_ref[...] = _rms(h_ref[...], gpre_ref[...]).astype(BF16)
        acc_ref[...] = jnp.zeros_like(acc_ref)

    xn = xn_ref[...]
    gate = jnp.dot(xn, wg_ref[...], preferred_element_type=F32)
    up = jnp.dot(xn, wu_ref[...], preferred_element_type=F32)
    act = (gate * _sigmoid(gate) * up).astype(BF16)
    acc_ref[...] += jnp.dot(act, wd_ref[...], preferred_element_type=F32)

    @pl.when(j == pl.num_programs(1) - 1)
    def _():
        o_ref[...] = h_ref[...] + HALF * _rms(acc_ref[...], gpost_ref[...])


def _ffn(h, gpre, gpost, wg, wu, wd, layer, tm=512, tf=512):
    t, d = h.shape
    f = wg.shape[2]
    tm = min(tm, t)
    return pl.pallas_call(
        _ffn_kernel,
        grid=(t // tm, f // tf),
        in_specs=[pl.BlockSpec((tm, d), lambda i, j: (i, 0)),
                  pl.BlockSpec((1, d), lambda i, j: (0, 0)),
                  pl.BlockSpec((1, d), lambda i, j: (0, 0)),
                  pl.BlockSpec((None, d, tf), lambda i, j: (layer, 0, j)),
                  pl.BlockSpec((None, d, tf), lambda i, j: (layer, 0, j)),
                  pl.BlockSpec((None, tf, d), lambda i, j: (layer, j, 0))],
        out_specs=pl.BlockSpec((tm, d), lambda i, j: (i, 0)),
        out_shape=jax.ShapeDtypeStruct((t, d), F32),
        scratch_shapes=[pltpu.VMEM((tm, d), BF16), pltpu.VMEM((tm, d), F32)],
        compiler_params=_cparams("parallel", "arbitrary"),
        name="ffn",
    )(h, gpre, gpost, wg, wu, wd)


def _branch_kernel(yr_ref, yd_ref, ym_ref, gr_ref, gd_ref, gm_ref, wb_ref, o_ref):
    acc = _sigmoid(gr_ref[...].astype(F32)) * jnp.dot(yr_ref[...], wb_ref[0], preferred_element_type=F32)
    acc += _sigmoid(gd_ref[...].astype(F32)) * jnp.dot(yd_ref[...], wb_ref[1], preferred_element_type=F32)
    acc += _sigmoid(gm_ref[...].astype(F32)) * jnp.dot(ym_ref[...], wb_ref[2], preferred_element_type=F32)
    o_ref[...] = acc.astype(o_ref.dtype)


def _branch_merge(y_rw, y_dsa, y_ml, z_gate, wb, layer, tm=1024, tn=512):
    t = y_rw.shape[0]
    d = D_MODEL
    tm = min(tm, t)
    nj = d // tn
    yspec = pl.BlockSpec((tm, MIX_W), lambda i, j: (i, 0))
    gspec = lambda b: pl.BlockSpec((tm, tn), lambda i, j: (i, b * nj + j))
    return pl.pallas_call(
        _branch_kernel,
        grid=(t // tm, nj),
        in_specs=[yspec, yspec, yspec, gspec(0), gspec(1), gspec(2),
                  pl.BlockSpec((None, 3, MIX_W, tn), lambda i, j: (layer, 0, 0, j))],
        out_specs=pl.BlockSpec((tm, tn), lambda i, j: (i, j)),
        out_shape=jax.ShapeDtypeStruct((t, d), BF16),
        compiler_params=_cparams("parallel", "arbitrary"),
        name="branch_merge",
    )(y_rw, y_dsa, y_ml, z_gate, z_gate, z_gate, wb)


def _outproj_kernel(h_ref, x_ref, g_ref, w_ref, o_ref):
    mix = jnp.dot(x_ref[...], w_ref[...], preferred_element_type=F32)
    o_ref[...] = h_ref[...] + _rms(mix, g_ref[...])


def _outproj(h, x, g, w, layer, tm=512):
    t, d = h.shape
    tm = min(tm, t)
    return pl.pallas_call(
        _outproj_kernel,
        grid=(t // tm,),
        in_specs=[pl.BlockSpec((tm, d), lambda i: (i, 0)),
                  pl.BlockSpec((tm, d), lambda i: (i, 0)),
                  pl.BlockSpec((1, d), lambda i: (0, 0)),
                  pl.BlockSpec((None, d, d), lambda i: (layer, 0, 0))],
        out_specs=pl.BlockSpec((tm, d), lambda i: (i, 0)),
        out_shape=jax.ShapeDtypeStruct((t, d), F32),
        compiler_params=_cparams("parallel"),
        name="outproj",
    )(h, x, g, w)


def _ple_kernel(h_ref, p_ref, g6_ref, g7_ref, wg_ref, wp_ref, o_ref):
    h = h_ref[...]
    pg = _sigmoid(jnp.dot(_rms(h, g6_ref[...]).astype(BF16), wg_ref[...], preferred_element_type=F32))
    e = jnp.dot(p_ref[...].astype(BF16), wp_ref[...], preferred_element_type=F32)
    o_ref[...] = h + _rms(pg * e, g7_ref[...])


def _ple(h, p, g6, g7, wg, wp, layer, tm=512):
    t, d = h.shape
    tm = min(tm, t)
    return pl.pallas_call(
        _ple_kernel,
        grid=(t // tm,),
        in_specs=[pl.BlockSpec((tm, d), lambda i: (i, 0)),
                  pl.BlockSpec((None, tm, PLE_DIM), lambda i: (layer, i, 0)),
                  pl.BlockSpec((1, d), lambda i: (0, 0)),
                  pl.BlockSpec((1, d), lambda i: (0, 0)),
                  pl.BlockSpec((None, d, d), lambda i: (layer, 0, 0)),
                  pl.BlockSpec((None, PLE_DIM, d), lambda i: (layer, 0, 0))],
        out_specs=pl.BlockSpec((tm, d), lambda i: (i, 0)),
        out_shape=jax.ShapeDtypeStruct((t, d), F32),
        compiler_params=_cparams("parallel"),
        name="ple",
    )(h, p, g6, g7, wg, wp)


def _log_sigmoid(x):
    return jnp.minimum(x, 0.0) - jnp.log(1.0 + jnp.exp(-jnp.abs(x)))


def _mlstm_kernel(z_ref, gt_ref, conv_ref, bcol_ref, brow_ref, hng_ref, tri_ref, y_ref,
                  c_ref, n_ref, m_ref, prev_ref):
    i = pl.program_id(0)
    ln = z_ref.shape[0]
    nh, dk, dv = ML_HEADS, ML_QK, ML_V
    nqk = 2 * nh * dk

    @pl.when(i == 0)
    def _():
        c_ref[...] = jnp.zeros_like(c_ref)
        n_ref[...] = jnp.zeros_like(n_ref)
        m_ref[...] = jnp.zeros_like(m_ref)
        prev_ref[...] = jnp.zeros_like(prev_ref)

    x = z_ref[:, 0:nqk]
    prev = prev_ref[...]
    row8 = lax.broadcasted_iota(jnp.int32, (8, 1), 0)
    w = conv_ref[...]
    acc = x * w[ML_CONV - 1:ML_CONV, :]
    for d in range(1, ML_CONV):
        sh = pltpu.roll(x, d, 0)
        top = jnp.where(row8 < d, pltpu.roll(prev, d, 0), sh[0:8, :])
        sh = jnp.concatenate([top, sh[8:, :]], axis=0)
        acc = acc + sh * w[ML_CONV - 1 - d:ML_CONV - d, :]
    prev_ref[...] = x[ln - 8:ln, :]
    qk = acc * _sigmoid(acc)
    q = qk[:, 0:nh * dk] * (dk ** -0.5)
    k = qk[:, nh * dk:nqk]

    gcol = z_ref[:, nqk + 2 * MIX_W:nqk + 2 * MIX_W + 2 * nh] + bcol_ref[...]
    grow = gt_ref[...] + brow_ref[...]
    lcol = _log_sigmoid(gcol)
    lrow = _log_sigmoid(grow)
    tri = tri_ref[...]
    c3 = _split3(lcol)
    cum_col = sum(jnp.dot(tri, t3, preferred_element_type=F32) for t3 in c3)
    r3 = _split3(lrow)
    cum_row = sum(lax.dot_general(t3, tri, (((1,), (1,)), ((), ())), preferred_element_type=F32) for t3 in r3)

    ti = lax.broadcasted_iota(jnp.int32, (ln, ln), 0)
    si = lax.broadcasted_iota(jnp.int32, (ln, ln), 1)
    tril = si <= ti
    for h in range(nh):
        g_col = cum_col[:, nh + h:nh + h + 1]
        g_row = cum_row[nh + h:nh + h + 1, :]
        i_col = gcol[:, h:h + 1]
        i_row = grow[h:h + 1, :]
        m_st = m_ref[h, 0:1, 0:1]
        c_st = c_ref[h]
        n_st = n_ref[h, 0:1, :]
        qh = q[:, h * dk:(h + 1) * dk]
        kh = k[:, h * dk:(h + 1) * dk]
        vh = z_ref[:, nqk + h * dv:nqk + (h + 1) * dv]
        oh = z_ref[:, nqk + MIX_W + h * dv:nqk + MIX_W + (h + 1) * dv]
        a_log = g_col + m_st
        d_log = jnp.where(tril, g_col - g_row + i_row, -jnp.inf)
        m_q = jnp.maximum(a_log, jnp.max(d_log, axis=-1, keepdims=True))
        inter = jnp.exp(a_log - m_q)
        s_w = lax.dot_general(qh.astype(BF16), kh.astype(BF16), (((1,), (1,)), ((), ())),
                              preferred_element_type=F32) * jnp.exp(d_log - m_q)
        num = inter * _dot(qh, c_st) + _dot(s_w, vh)
        den = inter * jnp.sum(qh * n_st, axis=-1, keepdims=True) + jnp.sum(s_w, axis=-1, keepdims=True)
        hh = num / jnp.maximum(jnp.abs(den), jnp.exp(-m_q))
        g_tot = g_col[ln - 1:ln, :]
        w_log = g_tot - g_col + i_col
        m_new = jnp.maximum(g_tot + m_st, jnp.max(w_log, axis=0, keepdims=True))
        decay = jnp.exp(g_tot + m_st - m_new)
        kw = kh * jnp.exp(w_log - m_new)
        c_ref[h] = decay * c_st + lax.dot_general(kw.astype(BF16), vh.astype(BF16), (((0,), (0,)), ((), ())),
                                                  preferred_element_type=F32)
        n_ref[h] = jnp.broadcast_to(decay * n_st + jnp.sum(kw, axis=0, keepdims=True), (8, dk))
        m_ref[h] = jnp.broadcast_to(m_new, (8, 128))
        hn = hh * lax.rsqrt(jnp.mean(hh * hh, axis=-1, keepdims=True) + EPS) * hng_ref[:, h * dv:(h + 1) * dv]
        y_ref[:, h * dv:(h + 1) * dv] = (_sigmoid(oh) * hn).astype(y_ref.dtype)


def _mlstm_call(z_ml, zt, gate_row0, conv_w, b_col, b_row, hn_g):
    t = z_ml.shape[0]
    ln = ML_CHUNK
    tri = jnp.asarray(np.tril(np.ones((ln, ln))), BF16)
    full = lambda a: pl.BlockSpec(a.shape, lambda i: (0,) * a.ndim)
    return pl.pallas_call(
        _mlstm_kernel,
        grid=(t // ln,),
        in_specs=[pl.BlockSpec((ln, ML_COLS_PAD), lambda i: (i, 0)),
                  pl.BlockSpec((8, ln), lambda i: (gate_row0 // 8, i)),
                  full(conv_w), full(b_col), full(b_row), full(hn_g), full(tri)],
        out_specs=pl.BlockSpec((ln, MIX_W), lambda i: (i, 0)),
        out_shape=jax.ShapeDtypeStruct((t, MIX_W), BF16),
        scratch_shapes=[pltpu.VMEM((ML_HEADS, ML_QK, ML_V), F32),
                        pltpu.VMEM((ML_HEADS, 8, ML_QK), F32),
                        pltpu.VMEM((ML_HEADS, 8, 128), F32),
                        pltpu.VMEM((8, 2 * ML_HEADS * ML_QK), F32)],
        compiler_params=_cparams("arbitrary"),
        name="mlstm",
    )(z_ml, zt, conv_w, b_col, b_row, hn_g, tri)


DSA_TQ = 256
DSA_GROUP = 4
INT_MIN = -2 ** 31
NEG = -1e30
LOG2E = math.log2(math.e)
F32_MIN_NORMAL = 2.0 ** -126
SHIFT_SLACK = 8.0


def _t5_bucket_bounds():
    max_exact = REL_BUCKETS // 2
    d = np.arange(1, 4 * REL_MAX_DIST, dtype=np.float64)
    large = max_exact + np.floor(np.log(d / max_exact) / math.log(REL_MAX_DIST / max_exact)
                                 * (REL_BUCKETS - max_exact)).astype(np.int64)
    bucket = np.where(d < max_exact, d.astype(np.int64), np.minimum(large, REL_BUCKETS - 1))
    return [int(d[np.argmax(bucket >= b)]) for b in range(max_exact + 1, REL_BUCKETS)]


def _bias_tile_kernel(rb_ref, o_ref):
    o = pl.program_id(0)
    h = pl.program_id(1)
    tq = o_ref.shape[-1]
    s_loc = lax.broadcasted_iota(jnp.int32, (tq, tq), 0)
    t_loc = lax.broadcasted_iota(jnp.int32, (tq, tq), 1)
    dist = o * tq + t_loc - s_loc
    max_exact = REL_BUCKETS // 2
    bucket = jnp.full((tq, tq), max_exact, jnp.int32)
    for bnd in _t5_bucket_bounds():
        bucket = bucket + (dist >= bnd).astype(jnp.int32)
    bucket = jnp.where(dist < max_exact, jnp.maximum(dist, 0), bucket)
    bias = jnp.zeros((tq, tq), F32)
    for b in range(REL_BUCKETS):
        bias = jnp.where(bucket == b, rb_ref[b, h], bias)
    o_ref[...] = (bias - rb_ref[REL_BUCKETS - 1, h]) * LOG2E


def _bias_tiles(rel_bias, tq):
    assert tq >= REL_MAX_DIST
    return pl.pallas_call(
        _bias_tile_kernel,
        grid=(3, DSA_HEADS),
        in_specs=[pl.BlockSpec(memory_space=pltpu.SMEM)],
        out_specs=pl.BlockSpec((None, None, tq, tq), lambda o, h: (o, h, 0, 0)),
        out_shape=jax.ShapeDtypeStruct((3, DSA_HEADS, tq, tq), F32),
        name="t5_bias_tiles",
    )(rel_bias)


def _dsa_prep_kernel(zt_ref, ckvt_ref, g_ref, gcol_ref, gk_ref, ck_ref, ki_ref, ct_ref):
    ckv = zt_ref[:, 0:DSA_KV_RANK]
    ck_ref[...] = _rms(ckv, g_ref[...]).astype(BF16)
    ki_ref[...] = _rms(zt_ref[:, DSA_KV_RANK:DSA_KV_RANK + IDX_DIM], gk_ref[...]).astype(BF16)
    ct = ckvt_ref[...]
    ms = jnp.mean(ct * ct, axis=0, keepdims=True)
    ct_ref[...] = (ct * lax.rsqrt(ms + EPS) * gcol_ref[...]).astype(BF16)


def _dsa_prep(z_dtok, zt, ckv_row0, g_row, g_col, gk_row):
    t = z_dtok.shape[0]
    tq = DSA_TQ
    full = lambda a: pl.BlockSpec(a.shape, lambda i: (0,) * a.ndim)
    return pl.pallas_call(
        _dsa_prep_kernel,
        grid=(t // tq,),
        in_specs=[pl.BlockSpec((tq, z_dtok.shape[1]), lambda i: (i, 0)),
                  pl.BlockSpec((DSA_KV_RANK, tq), lambda i: (ckv_row0 // DSA_KV_RANK, i)),
                  full(g_row), full(g_col), full(gk_row)],
        out_specs=[pl.BlockSpec((tq, DSA_KV_RANK), lambda i: (i, 0)),
                   pl.BlockSpec((tq, IDX_DIM), lambda i: (i, 0)),
                   pl.BlockSpec((None, DSA_KV_RANK, tq), lambda i: (i, 0, 0))],
        out_shape=[jax.ShapeDtypeStruct((t, DSA_KV_RANK), BF16),
                   jax.ShapeDtypeStruct((t, IDX_DIM), BF16),
                   jax.ShapeDtypeStruct((t // tq, DSA_KV_RANK, tq), BF16)],
        compiler_params=_cparams("parallel"),
        name="dsa_prep",
    )(z_dtok, zt, g_row, g_col, gk_row)


def _dsa_kernel(topk, qt_ref, qit_ref, wt_ref, ki_ref, ck_ref, ct_ref, wuk_ref, wuv_ref, bias_ref,
                o_ref, key_ref, acc_ref, m_ref, l_ref, qlat_ref, qi_ref, s_ref, tmax_ref, hi_ref):
    i = pl.program_id(0)
    tq = DSA_TQ
    nkb = i + 1
    nh, hd, r = DSA_HEADS, DSA_HEAD_DIM, DSA_KV_RANK

    for h in range(nh):
        ql = jnp.dot(wuk_ref[h], qt_ref[h * hd:(h + 1) * hd, :].astype(BF16), preferred_element_type=F32)
        qlat_ref[h] = (ql * (hd ** -0.5 * LOG2E)).astype(BF16)
    qi_ref[...] = qit_ref[...].astype(BF16)
    wrow = wt_ref[...] * ((IDX_DIM ** -0.5) * (IDX_HEADS ** -0.5))
    s_loc = lax.broadcasted_iota(jnp.int32, (tq, tq), 0)
    t_loc = lax.broadcasted_iota(jnp.int32, (tq, tq), 1)

    def score_blk(kb, carry):
        kblk = ki_ref[pl.ds(pl.multiple_of(kb * tq, tq), tq), :]
        acc = jnp.zeros((tq, tq), F32)
        for h in range(IDX_HEADS):
            rel = jnp.dot(kblk, qi_ref[h * IDX_DIM:(h + 1) * IDX_DIM, :], preferred_element_type=F32)
            acc = acc + wrow[h:h + 1, :] * jnp.maximum(rel, 0.0)
        acc = jnp.where(jnp.abs(acc) < F32_MIN_NORMAL, 0.0, acc)
        bits = pltpu.bitcast(acc, jnp.int32)
        key = jnp.where(bits < 0, bits ^ 0x7FFFFFFF, bits)
        causal = (kb * tq + s_loc) <= (i * tq + t_loc)
        key_ref[kb] = jnp.where(causal, key, INT_MIN)
        hi = pltpu.bitcast(bits & jnp.int32(-65536), F32)
        hi_ref[kb] = jnp.where(causal, hi, -jnp.inf).astype(BF16)
        return carry

    lax.fori_loop(0, nkb, score_blk, 0)

    grp = DSA_GROUP
    ngrp = (i + grp) // grp

    def pad_blk(kb, carry):
        key_ref[kb] = jnp.full((tq, tq), INT_MIN, jnp.int32)
        hi_ref[kb] = jnp.full((tq, tq), -jnp.inf, BF16)
        return carry

    lax.fori_loop(nkb, ngrp * grp, pad_blk, 0)

    def count_ge(cand):
        def body(g, cnt):
            for j in range(grp):
                ge = (key_ref[g * grp + j] >= cand).astype(jnp.int32)
                cnt = cnt + jnp.sum(ge.reshape(tq // 8, 8, tq), axis=0)
            return cnt
        cnt = lax.fori_loop(0, ngrp, body, jnp.zeros((8, tq), jnp.int32))
        return jnp.sum(cnt, axis=0, keepdims=True)

    few = (i * tq + t_loc[0:1, :]) < topk

    pk = tq // 16

    def count_hi(c_row):
        ct = jnp.broadcast_to(c_row.astype(BF16), (16, tq))
        one, nil = jnp.ones((), BF16), jnp.zeros((), BF16)

        def body(g, cnt):
            for j in range(grp):
                hb = hi_ref[g * grp + j].reshape(pk, 16, tq)
                part = jnp.where(hb[0] >= ct, one, nil)
                for r in range(1, pk):
                    part = part + jnp.where(hb[r] >= ct, one, nil)
                cnt = cnt + part.astype(F32)
            return cnt
        cnt = lax.fori_loop(0, ngrp, body, jnp.zeros((16, tq), F32))
        return jnp.sum(cnt, axis=0, keepdims=True)

    def hi_body(step, cur16):
        cand16 = cur16 + jnp.left_shift(jnp.int32(1), 15 - step)
        pat = jnp.where(cand16 >= 0, cand16, cand16 ^ 0x7FFF) & 0xFFFF
        c = pltpu.bitcast(jnp.left_shift(pat, 16), F32)
        c = jnp.where(jnp.abs(c) < F32_MIN_NORMAL, jnp.where(cand16 > 0, F32_MIN_NORMAL, 0.0), c)
        return jnp.where(count_hi(c) >= topk, cand16, cur16)

    cur16 = lax.fori_loop(0, 16, hi_body, jnp.full((1, tq), -32768, jnp.int32))

    def search_cond(c):
        return jnp.logical_and(c[0] < 32, c[3] > 0)

    def search_body(c):
        step, cur, cnt, _ = c
        cand = cur + jnp.left_shift(jnp.int32(1), 31 - step)
        cnt_c = count_ge(cand)
        take = cnt_c >= topk
        cur = jnp.where(take, cand, cur)
        cnt = jnp.where(take, cnt_c, cnt)
        pending = jnp.sum(jnp.where(few, 0, (cnt != topk).astype(jnp.int32)))
        return step + 1, cur, cnt, pending

    init = (jnp.int32(16), cur16 * 65536, jnp.full((1, tq), -1, jnp.int32), jnp.int32(1))
    thr = lax.while_loop(search_cond, search_body, init)[1]
    thr = jnp.where(few, INT_MIN + 1, jnp.maximum(thr, INT_MIN + 1))

    m_ref[...] = jnp.full_like(m_ref, NEG)
    l_ref[...] = jnp.zeros_like(l_ref)
    acc_ref[...] = jnp.zeros_like(acc_ref)

    def att_grp(near, g, carry):
        excess = jnp.full((1, tq), NEG, F32)
        tmax = [None] * nh
        for j in range(grp):
            kb = g * grp + j
            addm = jnp.where(key_ref[kb] >= thr, 0.0, NEG)
            cblk = ck_ref[pl.ds(pl.multiple_of(kb * tq, tq), tq), :]
            for h in range(nh):
                s = jnp.dot(cblk, qlat_ref[h], preferred_element_type=F32) + addm
                if near:
                    s = s + bias_ref[jnp.clip(i - kb, 0, 2), h]
                s_ref[j * nh + h] = s
                tm = jnp.max(s, axis=0, keepdims=True)
                tmax[h] = tm if j == 0 else jnp.maximum(tmax[h], tm)
        for h in range(nh):
            tmax_ref[h] = tmax[h]
            excess = jnp.maximum(excess, tmax[h] - m_ref[h])

        @pl.when(jnp.max(excess) > SHIFT_SLACK)
        def _():
            for h in range(nh):
                m_old = m_ref[h]
                m_new = jnp.maximum(m_old, tmax_ref[h])
                corr = jnp.exp2(m_old - m_new)
                l_ref[h] = l_ref[h] * corr
                acc_ref[h] = acc_ref[h] * corr
                m_ref[h] = m_new

        ctg = jnp.concatenate([ct_ref[g * grp + j] for j in range(grp)], axis=1)
        for h in range(nh):
            m_h = m_ref[h]
            p = jnp.concatenate([jnp.exp2(s_ref[j * nh + h] - m_h) for j in range(grp)], axis=0)
            l_ref[h] = l_ref[h] + jnp.sum(p, axis=0, keepdims=True)
            acc_ref[h] = acc_ref[h] + jnp.dot(ctg, p.astype(BF16), preferred_element_type=F32)
        return carry

    nfar = jnp.maximum(i - 1, 0) // grp
    lax.fori_loop(0, nfar, functools.partial(att_grp, False), 0)
    lax.fori_loop(nfar, ngrp, functools.partial(att_grp, True), 0)

    for h in range(nh):
        o_lat = (acc_ref[h] / l_ref[h]).astype(BF16)
        out_t = jnp.dot(wuv_ref[h], o_lat, preferred_element_type=F32)
        o_ref[:, h * hd:(h + 1) * hd] = out_t.T.astype(o_ref.dtype)


def _dsa_call(zt, q_row0, qi_row0, w_row0, ki, ck, ct, wuk_t, wuv_t, bias_tiles, topk):
    t = zt.shape[1]
    tq = DSA_TQ
    nq = DSA_HEADS * DSA_HEAD_DIM
    ni = IDX_HEADS * IDX_DIM
    full = lambda a: pl.BlockSpec(a.shape, lambda i: (0,) * a.ndim, pipeline_mode=pl.Buffered(1))
    return pl.pallas_call(
        functools.partial(_dsa_kernel, topk),
        grid=(t // tq,),
        in_specs=[pl.BlockSpec((nq, tq), lambda i: (q_row0 // nq, i)),
                  pl.BlockSpec((ni, tq), lambda i: (qi_row0 // ni, i)),
                  pl.BlockSpec((IDX_HEADS, tq), lambda i: (w_row0 // IDX_HEADS, i)),
                  full(ki), full(ck), full(ct), full(wuk_t), full(wuv_t), full(bias_tiles)],
        out_specs=pl.BlockSpec((tq, MIX_W), lambda i: (i, 0)),
        out_shape=jax.ShapeDtypeStruct((t, MIX_W), BF16),
        scratch_shapes=[pltpu.VMEM((t // tq, tq, tq), jnp.int32),
                        pltpu.VMEM((DSA_HEADS, DSA_KV_RANK, tq), F32),
                        pltpu.VMEM((DSA_HEADS, 1, tq), F32),
                        pltpu.VMEM((DSA_HEADS, 1, tq), F32),
                        pltpu.VMEM((DSA_HEADS, DSA_KV_RANK, tq), BF16),
                        pltpu.VMEM((ni, tq), BF16),
                        pltpu.VMEM((DSA_GROUP * DSA_HEADS, tq, tq), F32),
                        pltpu.VMEM((DSA_HEADS, 1, tq), F32),
                        pltpu.VMEM((t // tq, tq, tq), BF16)],
        compiler_params=_cparams("arbitrary"),
        name="dsa",
    )(zt, zt, zt, ki, ck, ct, wuk_t, wuv_t, bias_tiles)


FEAT_Q0 = 0
FEAT_QI0 = FEAT_Q0 + DSA_HEADS * DSA_HEAD_DIM
FEAT_CKV0 = FEAT_QI0 + IDX_HEADS * IDX_DIM
FEAT_W0 = FEAT_CKV0 + DSA_KV_RANK
FEAT_MLG0 = FEAT_W0 + IDX_HEADS
FEAT_ROWS = 2560
DTOK_COLS = 384


def _pad_last(a, n):
    return jnp.pad(a, [(0, 0)] * (a.ndim - 1) + [(0, n - a.shape[-1])])


def kernel(x, p, norm_gains, ffn1_gate, ffn1_up, ffn1_down, w_in, rw_mu, rw_w0, rw_w_up, rw_a0, rw_a_up, rw_g_up, rw_k_k, rw_k_a, rw_r_k, rw_lnx_g, rw_lnx_b, dsa_ckv_g, dsa_kidx_g, dsa_w_uk, dsa_w_uv, rel_bias, ml_conv, ml_b_i, ml_b_f, ml_hn_g, w_branch, w_out, ffn2_gate, ffn2_up, ffn2_down, ple_proj, ple_gate):
    b, t, d = x.shape
    assert b == 1 and d == D_MODEL
    nl = w_in.shape[0]
    h = x.reshape(t, d)
    p = p.reshape(nl, t, PLE_DIM)
    bf = lambda a: a.astype(BF16)

    f1g, f1u, f1d = bf(ffn1_gate), bf(ffn1_up), bf(ffn1_down)
    f2g, f2u, f2d = bf(ffn2_gate), bf(ffn2_up), bf(ffn2_down)
    wb, wo, wpg, wpp = bf(w_branch), bf(w_out), bf(ple_gate), bf(ple_proj)
    o_dsa = RW_COLS
    o_ml = o_dsa + DSA_COLS
    o_gate = o_ml + ML_COLS
    w_rw = bf(_pad_last(w_in[:, :, 0:RW_COLS], RW_COLS_PAD))
    w_ml = bf(_pad_last(w_in[:, :, o_ml:o_gate], ML_COLS_PAD))
    w_gate = bf(w_in[:, :, o_gate:])
    nq = DSA_HEADS * DSA_HEAD_DIM
    ni = IDX_HEADS * IDX_DIM
    w_q = w_in[:, :, o_dsa:o_dsa + nq]
    w_ckv = w_in[:, :, o_dsa + nq:o_dsa + nq + DSA_KV_RANK]
    w_qi = w_in[:, :, o_dsa + nq + DSA_KV_RANK:o_dsa + nq + DSA_KV_RANK + ni]
    w_ki = w_in[:, :, o_dsa + nq + DSA_KV_RANK + ni:o_dsa + nq + DSA_KV_RANK + ni + IDX_DIM]
    w_wi = w_in[:, :, o_dsa + DSA_COLS - IDX_HEADS:o_dsa + DSA_COLS]
    w_mlg = w_in[:, :, o_gate - 2 * ML_HEADS:o_gate]
    w_dtok = bf(_pad_last(jnp.concatenate([w_ckv, w_ki], axis=-1), DTOK_COLS))
    w_feat = bf(jnp.swapaxes(_pad_last(jnp.concatenate([w_q, w_qi, w_ckv, w_wi, w_mlg], axis=-1), FEAT_ROWS), 1, 2))
    mu = _pad_last(rw_mu, RW_COLS_PAD)
    wup, aup = bf(rw_w_up), bf(rw_a_up)
    gup = bf(jnp.pad(rw_g_up, ((0, 0), (0, RW_GATE_PAD - RW_GATE_LORA), (0, 0))))
    wuk_t = bf(jnp.swapaxes(dsa_w_uk, 2, 3))
    wuv_t = bf(jnp.swapaxes(dsa_w_uv, 2, 3))
    ml_b = jnp.concatenate([ml_b_i, ml_b_f], axis=-1)

    bias_tiles = _bias_tiles(rel_bias, DSA_TQ)
    topk = min(TOPK_MAX, t // 4)
    row = lambda a: a.reshape(1, -1)

    for i in range(nl):
        g = lambda k: norm_gains[i, k].reshape(1, d)
        h = _ffn(h, g(0), g(1), f1g, f1u, f1d, i)

        z_rw = _proj_tok(h, g(2), w_rw, i, tn=1152)
        z_ml = _proj_tok(h, g(2), w_ml, i, tn=640)
        z_gate = _proj_tok(h, g(2), w_gate, i, tn=1024, out_dtype=BF16)
        z_dtok = _proj_tok(h, g(2), w_dtok, i, tn=DTOK_COLS)
        zt = _proj_feat(h, g(2), w_feat, i, tn=1280)

        y_rw = _rwkv_call(z_rw, row(mu[i]), row(rw_w0[i]), wup[i], row(rw_a0[i]), aup[i], gup[i],
                          row(rw_k_k[i]), row(rw_k_a[i]), row(rw_r_k[i]), row(rw_lnx_g[i]), row(rw_lnx_b[i]))
        ck, ki, ct = _dsa_prep(z_dtok, zt, FEAT_CKV0, row(dsa_ckv_g[i]), dsa_ckv_g[i].reshape(-1, 1),
                               row(dsa_kidx_g[i]))
        y_dsa = _dsa_call(zt, FEAT_Q0, FEAT_QI0, FEAT_W0, ki, ck, ct, wuk_t[i], wuv_t[i], bias_tiles, topk)
        y_ml = _mlstm_call(z_ml, zt, FEAT_MLG0, ml_conv[i], row(ml_b[i]), ml_b[i].reshape(-1, 1), row(ml_hn_g[i]))

        mixsum = _branch_merge(y_rw, y_dsa, y_ml, z_gate, wb, i)
        h = _outproj(h, mixsum, g(3), wo, i)
        h = _ffn(h, g(4), g(5), f2g, f2u, f2d, i)
        h = _ple(h, p, g(6), g(7), wpg, wpp, i)
    return h.reshape(b, t, d)
```

```python
import functools
import math

import jax
import jax.numpy as jnp
import numpy as np
from jax import lax
from jax.experimental import pallas as pl
from jax.experimental.pallas import tpu as pltpu

F32 = jnp.float32
BF16 = jnp.bfloat16

D_MODEL = 2048
DEPTH = 4
PLE_DIM = 256
D_FF = 5632
MIX_W = 1024
EPS = 1e-6
HALF = 0.5

RW_HEAD = 64
RW_HEADS = 16
RW_DECAY_LORA = 64
RW_AAA_LORA = 64
RW_GATE_LORA = 160
RW_LNX_EPS = 64e-5
RW_COLS = 3 * MIX_W + RW_DECAY_LORA + RW_AAA_LORA + RW_GATE_LORA
RW_COLS_PAD = 3456
RW_GATE_PAD = RW_COLS_PAD - (3 * MIX_W + RW_DECAY_LORA + RW_AAA_LORA)

DSA_HEAD_DIM = 128
DSA_HEADS = 8
DSA_KV_RANK = 256
IDX_HEADS = 16
IDX_DIM = 64
TOPK_MAX = 256
DSA_COLS = DSA_HEADS * DSA_HEAD_DIM + DSA_KV_RANK + IDX_HEADS * IDX_DIM + IDX_DIM + IDX_HEADS

ML_HEADS = 4
ML_V = 256
ML_QK = 128
ML_CONV = 4
ML_CHUNK = 128
ML_COLS = 2 * ML_HEADS * ML_QK + 2 * MIX_W + 2 * ML_HEADS
ML_COLS_PAD = 3200

REL_BUCKETS = 32
REL_MAX_DIST = 128

VMEM_LIMIT_BYTES = 56 * 1024 * 1024

RW_CHUNK = 64
RW_BLOCK = 256


def _cparams(*sem):
    return pltpu.CompilerParams(dimension_semantics=sem, vmem_limit_bytes=VMEM_LIMIT_BYTES)


def _rms(x, g, eps=EPS):
    ms = jnp.mean(x * x, axis=-1, keepdims=True)
    return x * lax.rsqrt(ms + eps) * g


def _sigmoid(x):
    return 1.0 / (1.0 + jnp.exp(-x))


def _softplus(x):
    return jnp.maximum(x, 0.0) + jnp.log(1.0 + jnp.exp(-jnp.abs(x)))


def _split3(x):
    hi = x.astype(BF16)
    r1 = x - hi.astype(F32)
    mid = r1.astype(BF16)
    lo = (r1 - mid.astype(F32)).astype(BF16)
    return hi, mid, lo


def _dot(a, b):
    return jnp.dot(a.astype(BF16), b.astype(BF16), preferred_element_type=F32)


def _bdot(a, b, ca, cb):
    return lax.dot_general(a.astype(BF16), b.astype(BF16), (((ca,), (cb,)), ((0,), (0,))),
                           preferred_element_type=F32)


def _rwkv_kernel(z_ref, zprev_ref, mu_ref, w0_ref, wup_ref, a0_ref, aup_ref, gup_ref,
                 kk_ref, ka_ref, rk_ref, lng_ref, lnb_ref, tri_ref, bd_ref,
                 y_ref,
                 s_ref, at_ref, rt_ref, bt_ref, kt_ref, v_ref, wc_ref, y3_ref):
    i = pl.program_id(0)
    tb = z_ref.shape[0]
    nh, hd, c = RW_HEADS, RW_HEAD, RW_CHUNK
    nchunk = tb // c

    @pl.when(i == 0)
    def _():
        s_ref[...] = jnp.zeros_like(s_ref)

    z = z_ref[...]
    row = lax.broadcasted_iota(jnp.int32, (tb, 1), 0)
    prev_row = jnp.where(i == 0, 0.0, zprev_ref[7:8, :])
    shifted = jnp.where(row == 0, prev_row, pltpu.roll(z, 1, 0))
    zl = z + (shifted - z) * mu_ref[...]
    m = MIX_W
    r = zl[:, 0:m]
    k = zl[:, m:2 * m]
    v = zl[:, 2 * m:3 * m]
    wd = zl[:, 3 * m:3 * m + 64]
    ad = zl[:, 3 * m + 64:3 * m + 128]
    gd = zl[:, 3 * m + 128:]
    wpre = w0_ref[...] + _dot(jnp.tanh(wd), wup_ref[...])
    lw = -jnp.exp(-_softplus(-wpre) - 0.5)
    a = _sigmoid(a0_ref[...] + _dot(ad, aup_ref[...]))
    g = _dot(_sigmoid(gd), gup_ref[...])
    kkr = k * kk_ref[...]
    k2 = k * (1.0 + (a - 1.0) * ka_ref[...])
    rk2 = r * k2 * rk_ref[...]

    tri = tri_ref[...]
    h3, m3, l3 = _split3(lw)
    cum = (jnp.dot(tri, h3, preferred_element_type=F32) + jnp.dot(tri, m3, preferred_element_type=F32)
           + jnp.dot(tri, l3, preferred_element_type=F32))
    wc = jnp.exp(cum)
    winv = jnp.exp(-cum)
    wprev = jnp.exp(cum - lw)

    def stack(x):
        return jnp.stack([x[:, h * hd:(h + 1) * hd] for h in range(nh)], axis=0)

    bd = bd_ref[...]
    nb = bd.shape[0]

    def headsum(x):
        hi = x.astype(BF16)
        lo = (x - hi.astype(F32)).astype(BF16)
        return jnp.concatenate(
            [jnp.dot(hi[:, j:j + nb], bd, preferred_element_type=F32)
             + jnp.dot(lo[:, j:j + nb], bd, preferred_element_type=F32) for j in range(0, m, nb)], axis=1)

    kk = kkr / jnp.maximum(jnp.sqrt(headsum(kkr * kkr)), 1e-12)
    at_ref[...] = stack(-kk * wprev)
    bt_ref[...] = stack(kk * a * winv)
    kt_ref[...] = stack(k2 * winv)
    wc_ref[...] = stack(wc)
    rt_ref[...] = stack(r * wc)
    v_ref[...] = stack(v)

    ti = lax.broadcasted_iota(jnp.int32, (1, c, c), 1)
    si = lax.broadcasted_iota(jnp.int32, (1, c, c), 2)
    strict = si < ti
    incl = si <= ti
    eye = (si == ti).astype(F32)

    def chunk(ci, carry):
        sl = pl.ds(pl.multiple_of(ci * c, c), c)
        at = at_ref[:, sl, :]
        rt = rt_ref[:, sl, :]
        bt = bt_ref[:, sl, :]
        kt = kt_ref[:, sl, :]
        vv = v_ref[:, sl, :]
        wcl = wc_ref[:, pl.ds(ci * c + (c - 1), 1), :]
        ab = _bdot(at, bt, 2, 2)
        ak = _bdot(at, kt, 2, 2)
        rb = _bdot(rt, bt, 2, 2)
        rkm = _bdot(rt, kt, 2, 2)
        lmat = jnp.where(strict, ab, 0.0)
        aks = jnp.where(strict, ak, 0.0)
        rbt = jnp.where(incl, rb, 0.0)
        rkt = jnp.where(incl, rkm, 0.0)
        tinv = eye + lmat
        lp = lmat
        for _ in range(int(math.log2(c)) - 1):
            lp = _bdot(lp, lp, 2, 1)
            tinv = tinv + _bdot(lp, tinv, 2, 1)
        xv = _bdot(aks, vv, 2, 1)
        p = _bdot(tinv, at, 2, 1)
        qm = _bdot(tinv, xv, 2, 1)
        gm = rt + _bdot(rbt, p, 2, 1)
        y0 = _bdot(rbt, qm, 2, 1) + _bdot(rkt, vv, 2, 1)
        s = s_ref[...]
        u = _bdot(p, s, 2, 2) + qm
        y3_ref[:, sl, :] = _bdot(gm, s, 2, 2) + y0
        s_ref[...] = s * wcl + _bdot(u, bt * wcl, 1, 1) + _bdot(vv, kt * wcl, 1, 1)
        return carry

    lax.fori_loop(0, nchunk, chunk, 0)

    y = jnp.concatenate([y3_ref[h] for h in range(nh)], axis=-1)
    yc = y - headsum(y) * (1.0 / hd)
    var = headsum(yc * yc) * (1.0 / hd)
    yn = yc * lax.rsqrt(var + RW_LNX_EPS)
    bonus = headsum(rk2) * v
    y_ref[...] = ((yn * lng_ref[...] + lnb_ref[...] + bonus) * g).astype(y_ref.dtype)


def _rwkv_call(z_rw, mu, w0, w_up, a0, a_up, g_up, k_k, k_a, r_k, lnx_g, lnx_b):
    t = z_rw.shape[0]
    tb = min(RW_BLOCK, t)
    c = RW_CHUNK
    tri = jnp.asarray(np.kron(np.eye(tb // c), np.tril(np.ones((c, c)))), BF16)
    bd = jnp.asarray(np.kron(np.eye(256 // RW_HEAD), np.ones((RW_HEAD, RW_HEAD))), BF16)
    row = lambda n: pl.BlockSpec((1, n), lambda i: (0, 0))
    full = lambda a: pl.BlockSpec(a.shape, lambda i: (0,) * a.ndim)
    st = lambda: pltpu.VMEM((RW_HEADS, tb, RW_HEAD), F32)
    return pl.pallas_call(
        _rwkv_kernel,
        grid=(t // tb,),
        in_specs=[
            pl.BlockSpec((tb, RW_COLS_PAD), lambda i: (i, 0)),
            pl.BlockSpec((8, RW_COLS_PAD), lambda i: (jnp.maximum(i * (tb // 8) - 1, 0), 0)),
            row(RW_COLS_PAD), row(MIX_W), full(w_up), row(MIX_W), full(a_up), full(g_up),
            row(MIX_W), row(MIX_W), row(MIX_W), row(MIX_W), row(MIX_W), full(tri), full(bd),
        ],
        out_specs=pl.BlockSpec((tb, MIX_W), lambda i: (i, 0)),
        out_shape=jax.ShapeDtypeStruct((t, MIX_W), BF16),
        scratch_shapes=[pltpu.VMEM((RW_HEADS, RW_HEAD, RW_HEAD), F32),
                        st(), st(), st(), st(), st(), st(), st()],
        compiler_params=_cparams("arbitrary"),
        name="rwkv7",
    )(z_rw, z_rw, mu, w0, w_up, a0, a_up, g_up, k_k, k_a, r_k, lnx_g, lnx_b, tri, bd)


def _proj_tok_kernel(h_ref, g_ref, w_ref, o_ref, xn_ref):
    @pl.when(pl.program_id(1) == 0)
    def _():
        xn_ref[...] = _rms(h_ref[...], g_ref[...]).astype(BF16)

    o_ref[...] = jnp.dot(xn_ref[...], w_ref[...], preferred_element_type=F32).astype(o_ref.dtype)


def _proj_tok(h, g, w, layer, tn, out_dtype=F32, tm=1024):
    t, d = h.shape
    n = w.shape[2]
    tm = min(tm, t)
    return pl.pallas_call(
        _proj_tok_kernel,
        grid=(t // tm, n // tn),
        in_specs=[pl.BlockSpec((tm, d), lambda i, j: (i, 0)),
                  pl.BlockSpec((1, d), lambda i, j: (0, 0)),
                  pl.BlockSpec((None, d, tn), lambda i, j: (layer, 0, j))],
        out_specs=pl.BlockSpec((tm, tn), lambda i, j: (i, j)),
        out_shape=jax.ShapeDtypeStruct((t, n), out_dtype),
        scratch_shapes=[pltpu.VMEM((tm, d), BF16)],
        compiler_params=_cparams("parallel", "arbitrary"),
        name="proj_tok",
    )(h, g, w)


def _proj_feat_kernel(h_ref, g_ref, wt_ref, o_ref, xn_ref):
    @pl.when(pl.program_id(1) == 0)
    def _():
        xn_ref[...] = _rms(h_ref[...], g_ref[...]).astype(BF16)

    o_ref[...] = lax.dot_general(wt_ref[...], xn_ref[...], (((1,), (1,)), ((), ())),
                                 preferred_element_type=F32)


def _proj_feat(h, g, wt, layer, tn, tm=1024):
    t, d = h.shape
    n = wt.shape[1]
    tm = min(tm, t)
    return pl.pallas_call(
        _proj_feat_kernel,
        grid=(t // tm, n // tn),
        in_specs=[pl.BlockSpec((tm, d), lambda i, j: (i, 0)),
                  pl.BlockSpec((1, d), lambda i, j: (0, 0)),
                  pl.BlockSpec((None, tn, d), lambda i, j: (layer, j, 0))],
        out_specs=pl.BlockSpec((tn, tm), lambda i, j: (j, i)),
        out_shape=jax.ShapeDtypeStruct((n, t), F32),
        scratch_shapes=[pltpu.VMEM((tm, d), BF16)],
        compiler_params=_cparams("parallel", "arbitrary"),
        name="proj_feat",
    )(h, g, wt)


def _ffn_kernel(h_ref, gpre_ref, gpost_ref, wg_ref, wu_ref, wd_ref, o_ref, xn_ref, acc_ref):
    j = pl.program_id(1)

    @pl.when(j == 0)
    def _():
        xn_ref[...] = _rms(h_ref[...], gpre_ref[...]).astype(BF16)
        acc_ref[...] = jnp.zeros_like(acc_ref)

    xn = xn_ref[...]
    gate = jnp.dot(xn, wg_ref[...], preferred_element_type=F32)
    up = jnp.dot(xn, wu_ref[...], preferred_element_type=F32)
    act = (gate * _sigmoid(gate) * up).astype(BF16)
    acc_ref[...] += jnp.dot(act, wd_ref[...], preferred_element_type=F32)

    @pl.when(j == pl.num_programs(1) - 1)
    def _():
        o_ref[...] = h_ref[...] + HALF * _rms(acc_ref[...], gpost_ref[...])


def _ffn(h, gpre, gpost, wg, wu, wd, layer, tm=512, tf=512):
    t, d = h.shape
    f = wg.shape[2]
    tm = min(tm, t)
    return pl.pallas_call(
        _ffn_kernel,
        grid=(t // tm, f // tf),
        in_specs=[pl.BlockSpec((tm, d), lambda i, j: (i, 0)),
                  pl.BlockSpec((1, d), lambda i, j: (0, 0)),
                  pl.BlockSpec((1, d), lambda i, j: (0, 0)),
                  pl.BlockSpec((None, d, tf), lambda i, j: (layer, 0, j)),
                  pl.BlockSpec((None, d, tf), lambda i, j: (layer, 0, j)),
                  pl.BlockSpec((None, tf, d), lambda i, j: (layer, j, 0))],
        out_specs=pl.BlockSpec((tm, d), lambda i, j: (i, 0)),
        out_shape=jax.ShapeDtypeStruct((t, d), F32),
        scratch_shapes=[pltpu.VMEM((tm, d), BF16), pltpu.VMEM((tm, d), F32)],
        compiler_params=_cparams("parallel", "arbitrary"),
        name="ffn",
    )(h, gpre, gpost, wg, wu, wd)


def _branch_kernel(yr_ref, yd_ref, ym_ref, gr_ref, gd_ref, gm_ref, wb_ref, o_ref):
    acc = _sigmoid(gr_ref[...].astype(F32)) * jnp.dot(yr_ref[...], wb_ref[0], preferred_element_type=F32)
    acc += _sigmoid(gd_ref[...].astype(F32)) * jnp.dot(yd_ref[...], wb_ref[1], preferred_element_type=F32)
    acc += _sigmoid(gm_ref[...].astype(F32)) * jnp.dot(ym_ref[...], wb_ref[2], preferred_element_type=F32)
    o_ref[...] = acc.astype(o_ref.dtype)


def _branch_merge(y_rw, y_dsa, y_ml, z_gate, wb, layer, tm=1024, tn=512):
    t = y_rw.shape[0]
    d = D_MODEL
    tm = min(tm, t)
    nj = d // tn
    yspec = pl.BlockSpec((tm, MIX_W), lambda i, j: (i, 0))
    gspec = lambda b: pl.BlockSpec((tm, tn), lambda i, j: (i, b * nj + j))
    return pl.pallas_call(
        _branch_kernel,
        grid=(t // tm, nj),
        in_specs=[yspec, yspec, yspec, gspec(0), gspec(1), gspec(2),
                  pl.BlockSpec((None, 3, MIX_W, tn), lambda i, j: (layer, 0, 0, j))],
        out_specs=pl.BlockSpec((tm, tn), lambda i, j: (i, j)),
        out_shape=jax.ShapeDtypeStruct((t, d), BF16),
        compiler_params=_cparams("parallel", "arbitrary"),
        name="branch_merge",
    )(y_rw, y_dsa, y_ml, z_gate, z_gate, z_gate, wb)


def _outproj_kernel(h_ref, x_ref, g_ref, w_ref, o_ref):
    mix = jnp.dot(x_ref[...], w_ref[...], preferred_element_type=F32)
    o_ref[...] = h_ref[...] + _rms(mix, g_ref[...])


def _outproj(h, x, g, w, layer, tm=512):
    t, d = h.shape
    tm = min(tm, t)
    return pl.pallas_call(
        _outproj_kernel,
        grid=(t // tm,),
        in_specs=[pl.BlockSpec((tm, d), lambda i: (i, 0)),
                  pl.BlockSpec((tm, d), lambda i: (i, 0)),
                  pl.BlockSpec((1, d), lambda i: (0, 0)),
                  pl.BlockSpec((None, d, d), lambda i: (layer, 0, 0))],
        out_specs=pl.BlockSpec((tm, d), lambda i: (i, 0)),
        out_shape=jax.ShapeDtypeStruct((t, d), F32),
        compiler_params=_cparams("parallel"),
        name="outproj",
    )(h, x, g, w)


def _ple_kernel(h_ref, p_ref, g6_ref, g7_ref, wg_ref, wp_ref, o_ref):
    h = h_ref[...]
    pg = _sigmoid(jnp.dot(_rms(h, g6_ref[...]).astype(BF16), wg_ref[...], preferred_element_type=F32))
    e = jnp.dot(p_ref[...].astype(BF16), wp_ref[...], preferred_element_type=F32)
    o_ref[...] = h + _rms(pg * e, g7_ref[...])


def _ple(h, p, g6, g7, wg, wp, layer, tm=512):
    t, d = h.shape
    tm = min(tm, t)
    return pl.pallas_call(
        _ple_kernel,
        grid=(t // tm,),
        in_specs=[pl.BlockSpec((tm, d), lambda i: (i, 0)),
                  pl.BlockSpec((None, tm, PLE_DIM), lambda i: (layer, i, 0)),
                  pl.BlockSpec((1, d), lambda i: (0, 0)),
                  pl.BlockSpec((1, d), lambda i: (0, 0)),
                  pl.BlockSpec((None, d, d), lambda i: (layer, 0, 0)),
                  pl.BlockSpec((None, PLE_DIM, d), lambda i: (layer, 0, 0))],
        out_specs=pl.BlockSpec((tm, d), lambda i: (i, 0)),
        out_shape=jax.ShapeDtypeStruct((t, d), F32),
        compiler_params=_cparams("parallel"),
        name="ple",
    )(h, p, g6, g7, wg, wp)


def _log_sigmoid(x):
    return jnp.minimum(x, 0.0) - jnp.log(1.0 + jnp.exp(-jnp.abs(x)))


def _mlstm_kernel(z_ref, gt_ref, conv_ref, bcol_ref, brow_ref, hng_ref, tri_ref, y_ref,
                  c_ref, n_ref, m_ref, prev_ref):
    i = pl.program_id(0)
    ln = z_ref.shape[0]
    nh, dk, dv = ML_HEADS, ML_QK, ML_V
    nqk = 2 * nh * dk

    @pl.when(i == 0)
    def _():
        c_ref[...] = jnp.zeros_like(c_ref)
        n_ref[...] = jnp.zeros_like(n_ref)
        m_ref[...] = jnp.zeros_like(m_ref)
        prev_ref[...] = jnp.zeros_like(prev_ref)

    x = z_ref[:, 0:nqk]
    prev = prev_ref[...]
    row8 = lax.broadcasted_iota(jnp.int32, (8, 1), 0)
    w = conv_ref[...]
    acc = x * w[ML_CONV - 1:ML_CONV, :]
    for d in range(1, ML_CONV):
        sh = pltpu.roll(x, d, 0)
        top = jnp.where(row8 < d, pltpu.roll(prev, d, 0), sh[0:8, :])
        sh = jnp.concatenate([top, sh[8:, :]], axis=0)
        acc = acc + sh * w[ML_CONV - 1 - d:ML_CONV - d, :]
    prev_ref[...] = x[ln - 8:ln, :]
    qk = acc * _sigmoid(acc)
    q = qk[:, 0:nh * dk] * (dk ** -0.5)
    k = qk[:, nh * dk:nqk]

    gcol = z_ref[:, nqk + 2 * MIX_W:nqk + 2 * MIX_W + 2 * nh] + bcol_ref[...]
    grow = gt_ref[...] + brow_ref[...]
    lcol = _log_sigmoid(gcol)
    lrow = _log_sigmoid(grow)
    tri = tri_ref[...]
    c3 = _split3(lcol)
    cum_col = sum(jnp.dot(tri, t3, preferred_element_type=F32) for t3 in c3)
    r3 = _split3(lrow)
    cum_row = sum(lax.dot_general(t3, tri, (((1,), (1,)), ((), ())), preferred_element_type=F32) for t3 in r3)

    ti = lax.broadcasted_iota(jnp.int32, (ln, ln), 0)
    si = lax.broadcasted_iota(jnp.int32, (ln, ln), 1)
    tril = si <= ti
    for h in range(nh):
        g_col = cum_col[:, nh + h:nh + h + 1]
        g_row = cum_row[nh + h:nh + h + 1, :]
        i_col = gcol[:, h:h + 1]
        i_row = grow[h:h + 1, :]
        m_st = m_ref[h, 0:1, 0:1]
        c_st = c_ref[h]
        n_st = n_ref[h, 0:1, :]
        qh = q[:, h * dk:(h + 1) * dk]
        kh = k[:, h * dk:(h + 1) * dk]
        vh = z_ref[:, nqk + h * dv:nqk + (h + 1) * dv]
        oh = z_ref[:, nqk + MIX_W + h * dv:nqk + MIX_W + (h + 1) * dv]
        a_log = g_col + m_st
        d_log = jnp.where(tril, g_col - g_row + i_row, -jnp.inf)
        m_q = jnp.maximum(a_log, jnp.max(d_log, axis=-1, keepdims=True))
        inter = jnp.exp(a_log - m_q)
        s_w = lax.dot_general(qh.astype(BF16), kh.astype(BF16), (((1,), (1,)), ((), ())),
                              preferred_element_type=F32) * jnp.exp(d_log - m_q)
        num = inter * _dot(qh, c_st) + _dot(s_w, vh)
        den = inter * jnp.sum(qh * n_st, axis=-1, keepdims=True) + jnp.sum(s_w, axis=-1, keepdims=True)
        hh = num / jnp.maximum(jnp.abs(den), jnp.exp(-m_q))
        g_tot = g_col[ln - 1:ln, :]
        w_log = g_tot - g_col + i_col
        m_new = jnp.maximum(g_tot + m_st, jnp.max(w_log, axis=0, keepdims=True))
        decay = jnp.exp(g_tot + m_st - m_new)
        kw = kh * jnp.exp(w_log - m_new)
        c_ref[h] = decay * c_st + lax.dot_general(kw.astype(BF16), vh.astype(BF16), (((0,), (0,)), ((), ())),
                                                  preferred_element_type=F32)
        n_ref[h] = jnp.broadcast_to(decay * n_st + jnp.sum(kw, axis=0, keepdims=True), (8, dk))
        m_ref[h] = jnp.broadcast_to(m_new, (8, 128))
        hn = hh * lax.rsqrt(jnp.mean(hh * hh, axis=-1, keepdims=True) + EPS) * hng_ref[:, h * dv:(h + 1) * dv]
        y_ref[:, h * dv:(h + 1) * dv] = (_sigmoid(oh) * hn).astype(y_ref.dtype)


def _mlstm_call(z_ml, zt, gate_row0, conv_w, b_col, b_row, hn_g):
    t = z_ml.shape[0]
    ln = ML_CHUNK
    tri = jnp.asarray(np.tril(np.ones((ln, ln))), BF16)
    full = lambda a: pl.BlockSpec(a.shape, lambda i: (0,) * a.ndim)
    return pl.pallas_call(
        _mlstm_kernel,
        grid=(t // ln,),
        in_specs=[pl.BlockSpec((ln, ML_COLS_PAD), lambda i: (i, 0)),
                  pl.BlockSpec((8, ln), lambda i: (gate_row0 // 8, i)),
                  full(conv_w), full(b_col), full(b_row), full(hn_g), full(tri)],
        out_specs=pl.BlockSpec((ln, MIX_W), lambda i: (i, 0)),
        out_shape=jax.ShapeDtypeStruct((t, MIX_W), BF16),
        scratch_shapes=[pltpu.VMEM((ML_HEADS, ML_QK, ML_V), F32),
                        pltpu.VMEM((ML_HEADS, 8, ML_QK), F32),
                        pltpu.VMEM((ML_HEADS, 8, 128), F32),
                        pltpu.VMEM((8, 2 * ML_HEADS * ML_QK), F32)],
        compiler_params=_cparams("arbitrary"),
        name="mlstm",
    )(z_ml, zt, conv_w, b_col, b_row, hn_g, tri)


DSA_TQ = 256
DSA_GROUP = 4
INT_MIN = -2 ** 31
NEG = -1e30
LOG2E = math.log2(math.e)
F32_MIN_NORMAL = 2.0 ** -126
SHIFT_SLACK = 8.0


def _t5_bucket_bounds():
    max_exact = REL_BUCKETS // 2
    d = np.arange(1, 4 * REL_MAX_DIST, dtype=np.float64)
    large = max_exact + np.floor(np.log(d / max_exact) / math.log(REL_MAX_DIST / max_exact)
                                 * (REL_BUCKETS - max_exact)).astype(np.int64)
    bucket = np.where(d < max_exact, d.astype(np.int64), np.minimum(large, REL_BUCKETS - 1))
    return [int(d[np.argmax(bucket >= b)]) for b in range(max_exact + 1, REL_BUCKETS)]


def _bias_tile_kernel(rb_ref, o_ref):
    o = pl.program_id(0)
    h = pl.program_id(1)
    tq = o_ref.shape[-1]
    s_loc = lax.broadcasted_iota(jnp.int32, (tq, tq), 0)
    t_loc = lax.broadcasted_iota(jnp.int32, (tq, tq), 1)
    dist = o * tq + t_loc - s_loc
    max_exact = REL_BUCKETS // 2
    bucket = jnp.full((tq, tq), max_exact, jnp.int32)
    for bnd in _t5_bucket_bounds():
        bucket = bucket + (dist >= bnd).astype(jnp.int32)
    bucket = jnp.where(dist < max_exact, jnp.maximum(dist, 0), bucket)
    bias = jnp.zeros((tq, tq), F32)
    for b in range(REL_BUCKETS):
        bias = jnp.where(bucket == b, rb_ref[b, h], bias)
    o_ref[...] = (bias - rb_ref[REL_BUCKETS - 1, h]) * LOG2E


def _bias_tiles(rel_bias, tq):
    assert tq >= REL_MAX_DIST
    return pl.pallas_call(
        _bias_tile_kernel,
        grid=(3, DSA_HEADS),
        in_specs=[pl.BlockSpec(memory_space=pltpu.SMEM)],
        out_specs=pl.BlockSpec((None, None, tq, tq), lambda o, h: (o, h, 0, 0)),
        out_shape=jax.ShapeDtypeStruct((3, DSA_HEADS, tq, tq), F32),
        name="t5_bias_tiles",
    )(rel_bias)


def _dsa_prep_kernel(zt_ref, ckvt_ref, g_ref, gcol_ref, gk_ref, ck_ref, ki_ref, ct_ref):
    ckv = zt_ref[:, 0:DSA_KV_RANK]
    ck_ref[...] = _rms(ckv, g_ref[...]).astype(BF16)
    ki_ref[...] = _rms(zt_ref[:, DSA_KV_RANK:DSA_KV_RANK + IDX_DIM], gk_ref[...]).astype(BF16)
    ct = ckvt_ref[...]
    ms = jnp.mean(ct * ct, axis=0, keepdims=True)
    ct_ref[...] = (ct * lax.rsqrt(ms + EPS) * gcol_ref[...]).astype(BF16)


def _dsa_prep(z_dtok, zt, ckv_row0, g_row, g_col, gk_row):
    t = z_dtok.shape[0]
    tq = DSA_TQ
    full = lambda a: pl.BlockSpec(a.shape, lambda i: (0,) * a.ndim)
    return pl.pallas_call(
        _dsa_prep_kernel,
        grid=(t // tq,),
        in_specs=[pl.BlockSpec((tq, z_dtok.shape[1]), lambda i: (i, 0)),
                  pl.BlockSpec((DSA_KV_RANK, tq), lambda i: (ckv_row0 // DSA_KV_RANK, i)),
                  full(g_row), full(g_col), full(gk_row)],
        out_specs=[pl.BlockSpec((tq, DSA_KV_RANK), lambda i: (i, 0)),
                   pl.BlockSpec((tq, IDX_DIM), lambda i: (i, 0)),
                   pl.BlockSpec((None, DSA_KV_RANK, tq), lambda i: (i, 0, 0))],
        out_shape=[jax.ShapeDtypeStruct((t, DSA_KV_RANK), BF16),
                   jax.ShapeDtypeStruct((t, IDX_DIM), BF16),
                   jax.ShapeDtypeStruct((t // tq, DSA_KV_RANK, tq), BF16)],
        compiler_params=_cparams("parallel"),
        name="dsa_prep",
    )(z_dtok, zt, g_row, g_col, gk_row)


def _dsa_kernel(topk, qt_ref, qit_ref, wt_ref, ki_ref, ck_ref, ct_ref, wuk_ref, wuv_ref, bias_ref,
                o_ref, key_ref, acc_ref, m_ref, l_ref, qlat_ref, qi_ref, s_ref, tmax_ref, hi_ref):
    i = pl.program_id(0)
    tq = DSA_TQ
    nkb = i + 1
    nh, hd, r = DSA_HEADS, DSA_HEAD_DIM, DSA_KV_RANK

    for h in range(nh):
        ql = jnp.dot(wuk_ref[h], qt_ref[h * hd:(h + 1) * hd, :].astype(BF16), preferred_element_type=F32)
        qlat_ref[h] = (ql * (hd ** -0.5 * LOG2E)).astype(BF16)
    qi_ref[...] = qit_ref[...].astype(BF16)
    wrow = wt_ref[...] * ((IDX_DIM ** -0.5) * (IDX_HEADS ** -0.5))
    s_loc = lax.broadcasted_iota(jnp.int32, (tq, tq), 0)
    t_loc = lax.broadcasted_iota(jnp.int32, (tq, tq), 1)

    def score_blk(kb, carry):
        kblk = ki_ref[pl.ds(pl.multiple_of(kb * tq, tq), tq), :]
        acc = jnp.zeros((tq, tq), F32)
        for h in range(IDX_HEADS):
            rel = jnp.dot(kblk, qi_ref[h * IDX_DIM:(h + 1) * IDX_DIM, :], preferred_element_type=F32)
            acc = acc + wrow[h:h + 1, :] * jnp.maximum(rel, 0.0)
        acc = jnp.where(jnp.abs(acc) < F32_MIN_NORMAL, 0.0, acc)
        bits = pltpu.bitcast(acc, jnp.int32)
        key = jnp.where(bits < 0, bits ^ 0x7FFFFFFF, bits)
        causal = (kb * tq + s_loc) <= (i * tq + t_loc)
        key_ref[kb] = jnp.where(causal, key, INT_MIN)
        hi = pltpu.bitcast(bits & jnp.int32(-65536), F32)
        hi_ref[kb] = jnp.where(causal, hi, -jnp.inf).astype(BF16)
        return carry

    def score_pair(g, carry):
        score_blk(2 * g, carry)
        return score_blk(2 * g + 1, carry)

    lax.fori_loop(0, nkb // 2, score_pair, 0)

    @pl.when(nkb % 2 == 1)
    def _():
        score_blk(nkb - 1, 0)

    grp = DSA_GROUP
    ngrp = (i + grp) // grp

    def pad_blk(kb, carry):
        key_ref[kb] = jnp.full((tq, tq), INT_MIN, jnp.int32)
        hi_ref[kb] = jnp.full((tq, tq), -jnp.inf, BF16)
        return carry

    lax.fori_loop(nkb, ngrp * grp, pad_blk, 0)

    def count_ge(cand):
        def body(g, cnt):
            for j in range(grp):
                ge = (key_ref[g * grp + j] >= cand).astype(jnp.int32)
                cnt = cnt + jnp.sum(ge.reshape(tq // 8, 8, tq), axis=0)
            return cnt
        cnt = lax.fori_loop(0, ngrp, body, jnp.zeros((8, tq), jnp.int32))
        return jnp.sum(cnt, axis=0, keepdims=True)

    few = (i * tq + t_loc[0:1, :]) < topk

    pk = tq // 16

    def count_hi(c_row):
        ct = jnp.broadcast_to(c_row.astype(BF16), (16, tq))
        one, nil = jnp.ones((), BF16), jnp.zeros((), BF16)

        def body(g, cnt):
            for j in range(grp):
                hb = hi_ref[g * grp + j].reshape(pk, 16, tq)
                part = jnp.where(hb[0] >= ct, one, nil)
                for r in range(1, pk):
                    part = part + jnp.where(hb[r] >= ct, one, nil)
                cnt = cnt + part.astype(F32)
            return cnt
        cnt = lax.fori_loop(0, ngrp, body, jnp.zeros((16, tq), F32))
        return jnp.sum(cnt, axis=0, keepdims=True)

    def hi_body(step, cur16):
        cand16 = cur16 + jnp.left_shift(jnp.int32(1), 15 - step)
        pat = jnp.where(cand16 >= 0, cand16, cand16 ^ 0x7FFF) & 0xFFFF
        c = pltpu.bitcast(jnp.left_shift(pat, 16), F32)
        c = jnp.where(jnp.abs(c) < F32_MIN_NORMAL, jnp.where(cand16 > 0, F32_MIN_NORMAL, 0.0), c)
        return jnp.where(count_hi(c) >= topk, cand16, cur16)

    cur16 = lax.fori_loop(0, 16, hi_body, jnp.full((1, tq), -32768, jnp.int32))

    def search_cond(c):
        return jnp.logical_and(c[0] < 32, c[3] > 0)

    def search_body(c):
        step, cur, cnt, _ = c
        cand = cur + jnp.left_shift(jnp.int32(1), 31 - step)
        cnt_c = count_ge(cand)
        take = cnt_c >= topk
        cur = jnp.where(take, cand, cur)
        cnt = jnp.where(take, cnt_c, cnt)
        pending = jnp.sum(jnp.where(few, 0, (cnt != topk).astype(jnp.int32)))
        return step + 1, cur, cnt, pending

    init = (jnp.int32(16), cur16 * 65536, jnp.full((1, tq), -1, jnp.int32), jnp.int32(1))
    thr = lax.while_loop(search_cond, search_body, init)[1]
    thr = jnp.where(few, INT_MIN + 1, jnp.maximum(thr, INT_MIN + 1))

    m_ref[...] = jnp.full_like(m_ref, NEG)
    l_ref[...] = jnp.zeros_like(l_ref)
    acc_ref[...] = jnp.zeros_like(acc_ref)

    def att_grp(near, g, carry):
        excess = jnp.full((1, tq), NEG, F32)
        tmax = [None] * nh
        for j in range(grp):
            kb = g * grp + j
            addm = jnp.where(key_ref[kb] >= thr, 0.0, NEG)
            cblk = ck_ref[pl.ds(pl.multiple_of(kb * tq, tq), tq), :]
            for h in range(nh):
                s = jnp.dot(cblk, qlat_ref[h], preferred_element_type=F32) + addm
                if near:
                    s = s + bias_ref[jnp.clip(i - kb, 0, 2), h]
                s_ref[j * nh + h] = s
                tm = jnp.max(s, axis=0, keepdims=True)
                tmax[h] = tm if j == 0 else jnp.maximum(tmax[h], tm)
        for h in range(nh):
            tmax_ref[h] = tmax[h]
            excess = jnp.maximum(excess, tmax[h] - m_ref[h])

        @pl.when(jnp.max(excess) > SHIFT_SLACK)
        def _():
            for h in range(nh):
                m_old = m_ref[h]
                m_new = jnp.maximum(m_old, tmax_ref[h])
                corr = jnp.exp2(m_old - m_new)
                l_ref[h] = l_ref[h] * corr
                acc_ref[h] = acc_ref[h] * corr
                m_ref[h] = m_new

        ctg = jnp.concatenate([ct_ref[g * grp + j] for j in range(grp)], axis=1)
        for h in range(nh):
            m_h = m_ref[h]
            p = jnp.concatenate([jnp.exp2(s_ref[j * nh + h] - m_h) for j in range(grp)], axis=0)
            l_ref[h] = l_ref[h] + jnp.sum(p, axis=0, keepdims=True)
            acc_ref[h] = acc_ref[h] + jnp.dot(ctg, p.astype(BF16), preferred_element_type=F32)
        return carry

    nfar = jnp.maximum(i - 1, 0) // grp
    lax.fori_loop(0, nfar, functools.partial(att_grp, False), 0)
    lax.fori_loop(nfar, ngrp, functools.partial(att_grp, True), 0)

    for h in range(nh):
        o_lat = (acc_ref[h] / l_ref[h]).astype(BF16)
        out_t = jnp.dot(wuv_ref[h], o_lat, preferred_element_type=F32)
        o_ref[:, h * hd:(h + 1) * hd] = out_t.T.astype(o_ref.dtype)


def _dsa_call(zt, q_row0, qi_row0, w_row0, ki, ck, ct, wuk_t, wuv_t, bias_tiles, topk):
    t = zt.shape[1]
    tq = DSA_TQ
    nq = DSA_HEADS * DSA_HEAD_DIM
    ni = IDX_HEADS * IDX_DIM
    full = lambda a: pl.BlockSpec(a.shape, lambda i: (0,) * a.ndim, pipeline_mode=pl.Buffered(1))
    return pl.pallas_call(
        functools.partial(_dsa_kernel, topk),
        grid=(t // tq,),
        in_specs=[pl.BlockSpec((nq, tq), lambda i: (q_row0 // nq, i)),
                  pl.BlockSpec((ni, tq), lambda i: (qi_row0 // ni, i)),
                  pl.BlockSpec((IDX_HEADS, tq), lambda i: (w_row0 // IDX_HEADS, i)),
                  full(ki), full(ck), full(ct), full(wuk_t), full(wuv_t), full(bias_tiles)],
        out_specs=pl.BlockSpec((tq, MIX_W), lambda i: (i, 0)),
        out_shape=jax.ShapeDtypeStruct((t, MIX_W), BF16),
        scratch_shapes=[pltpu.VMEM((t // tq, tq, tq), jnp.int32),
                        pltpu.VMEM((DSA_HEADS, DSA_KV_RANK, tq), F32),
                        pltpu.VMEM((DSA_HEADS, 1, tq), F32),
                        pltpu.VMEM((DSA_HEADS, 1, tq), F32),
                        pltpu.VMEM((DSA_HEADS, DSA_KV_RANK, tq), BF16),
                        pltpu.VMEM((ni, tq), BF16),
                        pltpu.VMEM((DSA_GROUP * DSA_HEADS, tq, tq), F32),
                        pltpu.VMEM((DSA_HEADS, 1, tq), F32),
                        pltpu.VMEM((t // tq, tq, tq), BF16)],
        compiler_params=_cparams("arbitrary"),
        name="dsa",
    )(zt, zt, zt, ki, ck, ct, wuk_t, wuv_t, bias_tiles)


FEAT_Q0 = 0
FEAT_QI0 = FEAT_Q0 + DSA_HEADS * DSA_HEAD_DIM
FEAT_CKV0 = FEAT_QI0 + IDX_HEADS * IDX_DIM
FEAT_W0 = FEAT_CKV0 + DSA_KV_RANK
FEAT_MLG0 = FEAT_W0 + IDX_HEADS
FEAT_ROWS = 2560
DTOK_COLS = 384


def _pad_last(a, n):
    return jnp.pad(a, [(0, 0)] * (a.ndim - 1) + [(0, n - a.shape[-1])])


def kernel(x, p, norm_gains, ffn1_gate, ffn1_up, ffn1_down, w_in, rw_mu, rw_w0, rw_w_up, rw_a0, rw_a_up, rw_g_up, rw_k_k, rw_k_a, rw_r_k, rw_lnx_g, rw_lnx_b, dsa_ckv_g, dsa_kidx_g, dsa_w_uk, dsa_w_uv, rel_bias, ml_conv, ml_b_i, ml_b_f, ml_hn_g, w_branch, w_out, ffn2_gate, ffn2_up, ffn2_down, ple_proj, ple_gate):
    b, t, d = x.shape
    assert b == 1 and d == D_MODEL
    nl = w_in.shape[0]
    h = x.reshape(t, d)
    p = p.reshape(nl, t, PLE_DIM)
    bf = lambda a: a.astype(BF16)

    f1g, f1u, f1d = bf(ffn1_gate), bf(ffn1_up), bf(ffn1_down)
    f2g, f2u, f2d = bf(ffn2_gate), bf(ffn2_up), bf(ffn2_down)
    wb, wo, wpg, wpp = bf(w_branch), bf(w_out), bf(ple_gate), bf(ple_proj)
    o_dsa = RW_COLS
    o_ml = o_dsa + DSA_COLS
    o_gate = o_ml + ML_COLS
    w_rw = bf(_pad_last(w_in[:, :, 0:RW_COLS], RW_COLS_PAD))
    w_ml = bf(_pad_last(w_in[:, :, o_ml:o_gate], ML_COLS_PAD))
    w_gate = bf(w_in[:, :, o_gate:])
    nq = DSA_HEADS * DSA_HEAD_DIM
    ni = IDX_HEADS * IDX_DIM
    w_q = w_in[:, :, o_dsa:o_dsa + nq]
    w_ckv = w_in[:, :, o_dsa + nq:o_dsa + nq + DSA_KV_RANK]
    w_qi = w_in[:, :, o_dsa + nq + DSA_KV_RANK:o_dsa + nq + DSA_KV_RANK + ni]
    w_ki = w_in[:, :, o_dsa + nq + DSA_KV_RANK + ni:o_dsa + nq + DSA_KV_RANK + ni + IDX_DIM]
    w_wi = w_in[:, :, o_dsa + DSA_COLS - IDX_HEADS:o_dsa + DSA_COLS]
    w_mlg = w_in[:, :, o_gate - 2 * ML_HEADS:o_gate]
    w_dtok = bf(_pad_last(jnp.concatenate([w_ckv, w_ki], axis=-1), DTOK_COLS))
    w_feat = bf(jnp.swapaxes(_pad_last(jnp.concatenate([w_q, w_qi, w_ckv, w_wi, w_mlg], axis=-1), FEAT_ROWS), 1, 2))
    mu = _pad_last(rw_mu, RW_COLS_PAD)
    wup, aup = bf(rw_w_up), bf(rw_a_up)
    gup = bf(jnp.pad(rw_g_up, ((0, 0), (0, RW_GATE_PAD - RW_GATE_LORA), (0, 0))))
    wuk_t = bf(jnp.swapaxes(dsa_w_uk, 2, 3))
    wuv_t = bf(jnp.swapaxes(dsa_w_uv, 2, 3))
    ml_b = jnp.concatenate([ml_b_i, ml_b_f], axis=-1)

    bias_tiles = _bias_tiles(rel_bias, DSA_TQ)
    topk = min(TOPK_MAX, t // 4)
    row = lambda a: a.reshape(1, -1)

    for i in range(nl):
        g = lambda k: norm_gains[i, k].reshape(1, d)
        h = _ffn(h, g(0), g(1), f1g, f1u, f1d, i)

        z_rw = _proj_tok(h, g(2), w_rw, i, tn=1152)
        z_ml = _proj_tok(h, g(2), w_ml, i, tn=640)
        z_gate = _proj_tok(h, g(2), w_gate, i, tn=1024, out_dtype=BF16)
        z_dtok = _proj_tok(h, g(2), w_dtok, i, tn=DTOK_COLS)
        zt = _proj_feat(h, g(2), w_feat, i, tn=1280)

        y_rw = _rwkv_call(z_rw, row(mu[i]), row(rw_w0[i]), wup[i], row(rw_a0[i]), aup[i], gup[i],
                          row(rw_k_k[i]), row(rw_k_a[i]), row(rw_r_k[i]), row(rw_lnx_g[i]), row(rw_lnx_b[i]))
        ck, ki, ct = _dsa_prep(z_dtok, zt, FEAT_CKV0, row(dsa_ckv_g[i]), dsa_ckv_g[i].reshape(-1, 1),
                               row(dsa_kidx_g[i]))
        y_dsa = _dsa_call(zt, FEAT_Q0, FEAT_QI0, FEAT_W0, ki, ck, ct, wuk_t[i], wuv_t[i], bias_tiles, topk)
        y_ml = _mlstm_call(z_ml, zt, FEAT_MLG0, ml_conv[i], row(ml_b[i]), ml_b[i].reshape(-1, 1), row(ml_hn_g[i]))

        mixsum = _branch_merge(y_rw, y_dsa, y_ml, z_gate, wb, i)
        h = _outproj(h, mixsum, g(3), wo, i)
        h = _ffn(h, g(4), g(5), f2g, f2u, f2d, i)
        h = _ple(h, p, g(6), g(7), wpg, wpp, i)
    return h.reshape(b, t, d)
```

```python
import functools
import math

import jax
import jax.numpy as jnp
import numpy as np
from jax import lax
from jax.experimental import pallas as pl
from jax.experimental.pallas import tpu as pltpu

F32 = jnp.float32
BF16 = jnp.bfloat16

D_MODEL = 2048
DEPTH = 4
PLE_DIM = 256
D_FF = 5632
MIX_W = 1024
EPS = 1e-6
HALF = 0.5

RW_HEAD = 64
RW_HEADS = 16
RW_DECAY_LORA = 64
RW_AAA_LORA = 64
RW_GATE_LORA = 160
RW_LNX_EPS = 64e-5
RW_COLS = 3 * MIX_W + RW_DECAY_LORA + RW_AAA_LORA + RW_GATE_LORA
RW_COLS_PAD = 3456
RW_GATE_PAD = RW_COLS_PAD - (3 * MIX_W + RW_DECAY_LORA + RW_AAA_LORA)

DSA_HEAD_DIM = 128
DSA_HEADS = 8
DSA_KV_RANK = 256
IDX_HEADS = 16
IDX_DIM = 64
TOPK_MAX = 256
DSA_COLS = DSA_HEADS * DSA_HEAD_DIM + DSA_KV_RANK + IDX_HEADS * IDX_DIM + IDX_DIM + IDX_HEADS

ML_HEADS = 4
ML_V = 256
ML_QK = 128
ML_CONV = 4
ML_CHUNK = 128
ML_COLS = 2 * ML_HEADS * ML_QK + 2 * MIX_W + 2 * ML_HEADS
ML_COLS_PAD = 3200

REL_BUCKETS = 32
REL_MAX_DIST = 128

VMEM_LIMIT_BYTES = 56 * 1024 * 1024

RW_CHUNK = 64
RW_BLOCK = 256


def _cparams(*sem):
    return pltpu.CompilerParams(dimension_semantics=sem, vmem_limit_bytes=VMEM_LIMIT_BYTES)


def _rms(x, g, eps=EPS):
    ms = jnp.mean(x * x, axis=-1, keepdims=True)
    return x * lax.rsqrt(ms + eps) * g


def _sigmoid(x):
    return 1.0 / (1.0 + jnp.exp(-x))


def _softplus(x):
    return jnp.maximum(x, 0.0) + jnp.log(1.0 + jnp.exp(-jnp.abs(x)))


def _split3(x):
    hi = x.astype(BF16)
    r1 = x - hi.astype(F32)
    mid = r1.astype(BF16)
    lo = (r1 - mid.astype(F32)).astype(BF16)
    return hi, mid, lo


def _dot(a, b):
    return jnp.dot(a.astype(BF16), b.astype(BF16), preferred_element_type=F32)


def _bdot(a, b, ca, cb):
    return lax.dot_general(a.astype(BF16), b.astype(BF16), (((ca,), (cb,)), ((0,), (0,))),
                           preferred_element_type=F32)


def _rwkv_kernel(z_ref, zprev_ref, mu_ref, w0_ref, wup_ref, a0_ref, aup_ref, gup_ref,
                 kk_ref, ka_ref, rk_ref, lng_ref, lnb_ref, tri_ref, bd_ref,
                 y_ref,
                 s_ref, at_ref, rt_ref, bt_ref, kt_ref, v_ref, wc_ref, y3_ref):
    i = pl.program_id(0)
    tb = z_ref.shape[0]
    nh, hd, c = RW_HEADS, RW_HEAD, RW_CHUNK
    nchunk = tb // c

    @pl.when(i == 0)
    def _():
        s_ref[...] = jnp.zeros_like(s_ref)

    z = z_ref[...]
    row = lax.broadcasted_iota(jnp.int32, (tb, 1), 0)
    prev_row = jnp.where(i == 0, 0.0, zprev_ref[7:8, :])
    shifted = jnp.where(row == 0, prev_row, pltpu.roll(z, 1, 0))
    zl = z + (shifted - z) * mu_ref[...]
    m = MIX_W
    r = zl[:, 0:m]
    k = zl[:, m:2 * m]
    v = zl[:, 2 * m:3 * m]
    wd = zl[:, 3 * m:3 * m + 64]
    ad = zl[:, 3 * m + 64:3 * m + 128]
    gd = zl[:, 3 * m + 128:]
    wpre = w0_ref[...] + _dot(jnp.tanh(wd), wup_ref[...])
    lw = -jnp.exp(-_softplus(-wpre) - 0.5)
    a = _sigmoid(a0_ref[...] + _dot(ad, aup_ref[...]))
    g = _dot(_sigmoid(gd), gup_ref[...])
    kkr = k * kk_ref[...]
    k2 = k * (1.0 + (a - 1.0) * ka_ref[...])
    rk2 = r * k2 * rk_ref[...]

    tri = tri_ref[...]
    h3, m3, l3 = _split3(lw)
    cum = (jnp.dot(tri, h3, preferred_element_type=F32) + jnp.dot(tri, m3, preferred_element_type=F32)
           + jnp.dot(tri, l3, preferred_element_type=F32))
    wc = jnp.exp(cum)
    winv = jnp.exp(-cum)
    wprev = jnp.exp(cum - lw)

    def stack(x):
        return jnp.stack([x[:, h * hd:(h + 1) * hd] for h in range(nh)], axis=0)

    bd = bd_ref[...]
    nb = bd.shape[0]

    def headsum(x):
        hi = x.astype(BF16)
        lo = (x - hi.astype(F32)).astype(BF16)
        return jnp.concatenate(
            [jnp.dot(hi[:, j:j + nb], bd, preferred_element_type=F32)
             + jnp.dot(lo[:, j:j + nb], bd, preferred_element_type=F32) for j in range(0, m, nb)], axis=1)

    kk = kkr / jnp.maximum(jnp.sqrt(headsum(kkr * kkr)), 1e-12)
    at_ref[...] = stack(-kk * wprev)
    bt_ref[...] = stack(kk * a * winv)
    kt_ref[...] = stack(k2 * winv)
    wc_ref[...] = stack(wc)
    rt_ref[...] = stack(r * wc)
    v_ref[...] = stack(v)

    ti = lax.broadcasted_iota(jnp.int32, (1, c, c), 1)
    si = lax.broadcasted_iota(jnp.int32, (1, c, c), 2)
    strict = si < ti
    incl = si <= ti
    eye = (si == ti).astype(F32)

    def chunk(ci, carry):
        sl = pl.ds(pl.multiple_of(ci * c, c), c)
        at = at_ref[:, sl, :]
        rt = rt_ref[:, sl, :]
        bt = bt_ref[:, sl, :]
        kt = kt_ref[:, sl, :]
        vv = v_ref[:, sl, :]
        wcl = wc_ref[:, pl.ds(ci * c + (c - 1), 1), :]
        ab = _bdot(at, bt, 2, 2)
        ak = _bdot(at, kt, 2, 2)
        rb = _bdot(rt, bt, 2, 2)
        rkm = _bdot(rt, kt, 2, 2)
        lmat = jnp.where(strict, ab, 0.0)
        aks = jnp.where(strict, ak, 0.0)
        rbt = jnp.where(incl, rb, 0.0)
        rkt = jnp.where(incl, rkm, 0.0)
        tinv = eye + lmat
        lp = lmat
        for _ in range(int(math.log2(c)) - 1):
            lp = _bdot(lp, lp, 2, 1)
            tinv = tinv + _bdot(lp, tinv, 2, 1)
        xv = _bdot(aks, vv, 2, 1)
        p = _bdot(tinv, at, 2, 1)
        qm = _bdot(tinv, xv, 2, 1)
        gm = rt + _bdot(rbt, p, 2, 1)
        y0 = _bdot(rbt, qm, 2, 1) + _bdot(rkt, vv, 2, 1)
        s = s_ref[...]
        u = _bdot(p, s, 2, 2) + qm
        y3_ref[:, sl, :] = _bdot(gm, s, 2, 2) + y0
        s_ref[...] = s * wcl + _bdot(u, bt * wcl, 1, 1) + _bdot(vv, kt * wcl, 1, 1)
        return carry

    lax.fori_loop(0, nchunk, chunk, 0)

    y = jnp.concatenate([y3_ref[h] for h in range(nh)], axis=-1)
    yc = y - headsum(y) * (1.0 / hd)
    var = headsum(yc * yc) * (1.0 / hd)
    yn = yc * lax.rsqrt(var + RW_LNX_EPS)
    bonus = headsum(rk2) * v
    y_ref[...] = ((yn * lng_ref[...] + lnb_ref[...] + bonus) * g).astype(y_ref.dtype)


def _rwkv_call(z_rw, mu, w0, w_up, a0, a_up, g_up, k_k, k_a, r_k, lnx_g, lnx_b):
    t = z_rw.shape[0]
    tb = min(RW_BLOCK, t)
    c = RW_CHUNK
    tri = jnp.asarray(np.kron(np.eye(tb // c), np.tril(np.ones((c, c)))), BF16)
    bd = jnp.asarray(np.kron(np.eye(256 // RW_HEAD), np.ones((RW_HEAD, RW_HEAD))), BF16)
    row = lambda n: pl.BlockSpec((1, n), lambda i: (0, 0))
    full = lambda a: pl.BlockSpec(a.shape, lambda i: (0,) * a.ndim)
    st = lambda: pltpu.VMEM((RW_HEADS, tb, RW_HEAD), F32)
    return pl.pallas_call(
        _rwkv_kernel,
        grid=(t // tb,),
        in_specs=[
            pl.BlockSpec((tb, RW_COLS_PAD), lambda i: (i, 0)),
            pl.BlockSpec((8, RW_COLS_PAD), lambda i: (jnp.maximum(i * (tb // 8) - 1, 0), 0)),
            row(RW_COLS_PAD), row(MIX_W), full(w_up), row(MIX_W), full(a_up), full(g_up),
            row(MIX_W), row(MIX_W), row(MIX_W), row(MIX_W), row(MIX_W), full(tri), full(bd),
        ],
        out_specs=pl.BlockSpec((tb, MIX_W), lambda i: (i, 0)),
        out_shape=jax.ShapeDtypeStruct((t, MIX_W), BF16),
        scratch_shapes=[pltpu.VMEM((RW_HEADS, RW_HEAD, RW_HEAD), F32),
                        st(), st(), st(), st(), st(), st(), st()],
        compiler_params=_cparams("arbitrary"),
        name="rwkv7",
    )(z_rw, z_rw, mu, w0, w_up, a0, a_up, g_up, k_k, k_a, r_k, lnx_g, lnx_b, tri, bd)


def _proj_tok_kernel(h_ref, g_ref, w_ref, o_ref, xn_ref):
    @pl.when(pl.program_id(1) == 0)
    def _():
        xn_ref[...] = _rms(h_ref[...], g_ref[...]).astype(BF16)

    o_ref[...] = jnp.dot(xn_ref[...], w_ref[...], preferred_element_type=F32).astype(o_ref.dtype)


def _proj_tok(h, g, w, layer, tn, out_dtype=F32, tm=1024):
    t, d = h.shape
    n = w.shape[2]
    tm = min(tm, t)
    return pl.pallas_call(
        _proj_tok_kernel,
        grid=(t // tm, n // tn),
        in_specs=[pl.BlockSpec((tm, d), lambda i, j: (i, 0)),
                  pl.BlockSpec((1, d), lambda i, j: (0, 0)),
                  pl.BlockSpec((None, d, tn), lambda i, j: (layer, 0, j))],
        out_specs=pl.BlockSpec((tm, tn), lambda i, j: (i, j)),
        out_shape=jax.ShapeDtypeStruct((t, n), out_dtype),
        scratch_shapes=[pltpu.VMEM((tm, d), BF16)],
        compiler_params=_cparams("parallel", "arbitrary"),
        name="proj_tok",
    )(h, g, w)


def _proj_feat_kernel(h_ref, g_ref, wt_ref, o_ref, xn_ref):
    @pl.when(pl.program_id(1) == 0)
    def _():
        xn_ref[...] = _rms(h_ref[...], g_ref[...]).astype(BF16)

    o_ref[...] = lax.dot_general(wt_ref[...], xn_ref[...], (((1,), (1,)), ((), ())),
                                 preferred_element_type=F32)


def _proj_feat(h, g, wt, layer, tn, tm=1024):
    t, d = h.shape
    n = wt.shape[1]
    tm = min(tm, t)
    return pl.pallas_call(
        _proj_feat_kernel,
        grid=(t // tm, n // tn),
        in_specs=[pl.BlockSpec((tm, d), lambda i, j: (i, 0)),
                  pl.BlockSpec((1, d), lambda i, j: (0, 0)),
                  pl.BlockSpec((None, tn, d), lambda i, j: (layer, j, 0))],
        out_specs=pl.BlockSpec((tn, tm), lambda i, j: (j, i)),
        out_shape=jax.ShapeDtypeStruct((n, t), F32),
        scratch_shapes=[pltpu.VMEM((tm, d), BF16)],
        compiler_params=_cparams("parallel", "arbitrary"),
        name="proj_feat",
    )(h, g, wt)


def _ffn_kernel(h_ref, gpre_ref, gpost_ref, wg_ref, wu_ref, wd_ref, o_ref, xn_ref, acc_ref):
    j = pl.program_id(1)

    @pl.when(j == 0)
    def _():
        xn_ref[...] = _rms(h_ref[...], gpre_ref[...]).astype(BF16)
        acc_ref[...] = jnp.zeros_like(acc_ref)

    xn = xn_ref[...]
    gate = jnp.dot(xn, wg_ref[...], preferred_element_type=F32)
    up = jnp.dot(xn, wu_ref[...], preferred_element_type=F32)
    act = (gate * _sigmoid(gate) * up).astype(BF16)
    acc_ref[...] += jnp.dot(act, wd_ref[...], preferred_element_type=F32)

    @pl.when(j == pl.num_programs(1) - 1)
    def _():
        o_ref[...] = h_ref[...] + HALF * _rms(acc_ref[...], gpost_ref[...])


def _ffn(h, gpre, gpost, wg, wu, wd, layer, tm=512, tf=512):
    t, d = h.shape
    f = wg.shape[2]
    tm = min(tm, t)
    return pl.pallas_call(
        _ffn_kernel,
        grid=(t // tm, f // tf),
        in_specs=[pl.BlockSpec((tm, d), lambda i, j: (i, 0)),
                  pl.BlockSpec((1, d), lambda i, j: (0, 0)),
                  pl.BlockSpec((1, d), lambda i, j: (0, 0)),
                  pl.BlockSpec((None, d, tf), lambda i, j: (layer, 0, j)),
                  pl.BlockSpec((None, d, tf), lambda i, j: (layer, 0, j)),
                  pl.BlockSpec((None, tf, d), lambda i, j: (layer, j, 0))],
        out_specs=pl.BlockSpec((tm, d), lambda i, j: (i, 0)),
        out_shape=jax.ShapeDtypeStruct((t, d), F32),
        scratch_shapes=[pltpu.VMEM((tm, d), BF16), pltpu.VMEM((tm, d), F32)],
        compiler_params=_cparams("parallel", "arbitrary"),
        name="ffn",
    )(h, gpre, gpost, wg, wu, wd)


def _merge_out_kernel(yr_ref, yd_ref, ym_ref, gr_ref, gd_ref, gm_ref, wb_ref, h_ref, g_ref, wo_ref,
                      o_ref, mix_ref):
    j = pl.program_id(1)
    nj = mix_ref.shape[0]
    acc = _sigmoid(gr_ref[...].astype(F32)) * jnp.dot(yr_ref[...], wb_ref[0], preferred_element_type=F32)
    acc += _sigmoid(gd_ref[...].astype(F32)) * jnp.dot(yd_ref[...], wb_ref[1], preferred_element_type=F32)
    acc += _sigmoid(gm_ref[...].astype(F32)) * jnp.dot(ym_ref[...], wb_ref[2], preferred_element_type=F32)
    mix_ref[j] = acc.astype(BF16)

    @pl.when(j == nj - 1)
    def _():
        x = jnp.concatenate([mix_ref[k] for k in range(nj)], axis=1)
        mix = jnp.dot(x, wo_ref[...], preferred_element_type=F32)
        o_ref[...] = h_ref[...] + _rms(mix, g_ref[...])


def _merge_out(h, y_rw, y_dsa, y_ml, z_gate, g, wb, wo, layer, tm=512, tn=512):
    t, d = h.shape
    tm = min(tm, t)
    nj = d // tn
    yspec = pl.BlockSpec((tm, MIX_W), lambda i, j: (i, 0))
    gspec = lambda b: pl.BlockSpec((tm, tn), lambda i, j: (i, b * nj + j))
    return pl.pallas_call(
        _merge_out_kernel,
        grid=(t // tm, nj),
        in_specs=[yspec, yspec, yspec, gspec(0), gspec(1), gspec(2),
                  pl.BlockSpec((None, 3, MIX_W, tn), lambda i, j: (layer, 0, 0, j)),
                  pl.BlockSpec((tm, d), lambda i, j: (i, 0)),
                  pl.BlockSpec((1, d), lambda i, j: (0, 0)),
                  pl.BlockSpec((None, d, d), lambda i, j: (layer, 0, 0), pipeline_mode=pl.Buffered(1))],
        out_specs=pl.BlockSpec((tm, d), lambda i, j: (i, 0)),
        out_shape=jax.ShapeDtypeStruct((t, d), F32),
        scratch_shapes=[pltpu.VMEM((nj, tm, tn), BF16)],
        compiler_params=_cparams("parallel", "arbitrary"),
        name="merge_out",
    )(y_rw, y_dsa, y_ml, z_gate, z_gate, z_gate, wb, h, g, wo)


def _ple_kernel(h_ref, p_ref, g6_ref, g7_ref, wg_ref, wp_ref, o_ref):
    h = h_ref[...]
    pg = _sigmoid(jnp.dot(_rms(h, g6_ref[...]).astype(BF16), wg_ref[...], preferred_element_type=F32))
    e = jnp.dot(p_ref[...].astype(BF16), wp_ref[...], preferred_element_type=F32)
    o_ref[...] = h + _rms(pg * e, g7_ref[...])


def _ple(h, p, g6, g7, wg, wp, layer, tm=512):
    t, d = h.shape
    tm = min(tm, t)
    return pl.pallas_call(
        _ple_kernel,
        grid=(t // tm,),
        in_specs=[pl.BlockSpec((tm, d), lambda i: (i, 0)),
                  pl.BlockSpec((None, tm, PLE_DIM), lambda i: (layer, i, 0)),
                  pl.BlockSpec((1, d), lambda i: (0, 0)),
                  pl.BlockSpec((1, d), lambda i: (0, 0)),
                  pl.BlockSpec((None, d, d), lambda i: (layer, 0, 0)),
                  pl.BlockSpec((None, PLE_DIM, d), lambda i: (layer, 0, 0))],
        out_specs=pl.BlockSpec((tm, d), lambda i: (i, 0)),
        out_shape=jax.ShapeDtypeStruct((t, d), F32),
        compiler_params=_cparams("parallel"),
        name="ple",
    )(h, p, g6, g7, wg, wp)


def _log_sigmoid(x):
    return jnp.minimum(x, 0.0) - jnp.log(1.0 + jnp.exp(-jnp.abs(x)))


def _mlstm_kernel(z_ref, gt_ref, conv_ref, bcol_ref, brow_ref, hng_ref, tri_ref, y_ref,
                  c_ref, n_ref, m_ref, prev_ref):
    i = pl.program_id(0)
    ln = z_ref.shape[0]
    nh, dk, dv = ML_HEADS, ML_QK, ML_V
    nqk = 2 * nh * dk

    @pl.when(i == 0)
    def _():
        c_ref[...] = jnp.zeros_like(c_ref)
        n_ref[...] = jnp.zeros_like(n_ref)
        m_ref[...] = jnp.zeros_like(m_ref)
        prev_ref[...] = jnp.zeros_like(prev_ref)

    x = z_ref[:, 0:nqk]
    prev = prev_ref[...]
    row8 = lax.broadcasted_iota(jnp.int32, (8, 1), 0)
    w = conv_ref[...]
    acc = x * w[ML_CONV - 1:ML_CONV, :]
    for d in range(1, ML_CONV):
        sh = pltpu.roll(x, d, 0)
        top = jnp.where(row8 < d, pltpu.roll(prev, d, 0), sh[0:8, :])
        sh = jnp.concatenate([top, sh[8:, :]], axis=0)
        acc = acc + sh * w[ML_CONV - 1 - d:ML_CONV - d, :]
    prev_ref[...] = x[ln - 8:ln, :]
    qk = acc * _sigmoid(acc)
    q = qk[:, 0:nh * dk] * (dk ** -0.5)
    k = qk[:, nh * dk:nqk]

    gcol = z_ref[:, nqk + 2 * MIX_W:nqk + 2 * MIX_W + 2 * nh] + bcol_ref[...]
    grow = gt_ref[...] + brow_ref[...]
    lcol = _log_sigmoid(gcol)
    lrow = _log_sigmoid(grow)
    tri = tri_ref[...]
    c3 = _split3(lcol)
    cum_col = sum(jnp.dot(tri, t3, preferred_element_type=F32) for t3 in c3)
    r3 = _split3(lrow)
    cum_row = sum(lax.dot_general(t3, tri, (((1,), (1,)), ((), ())), preferred_element_type=F32) for t3 in r3)

    ti = lax.broadcasted_iota(jnp.int32, (ln, ln), 0)
    si = lax.broadcasted_iota(jnp.int32, (ln, ln), 1)
    tril = si <= ti
    for h in range(nh):
        g_col = cum_col[:, nh + h:nh + h + 1]
        g_row = cum_row[nh + h:nh + h + 1, :]
        i_col = gcol[:, h:h + 1]
        i_row = grow[h:h + 1, :]
        m_st = m_ref[h, 0:1, 0:1]
        c_st = c_ref[h]
        n_st = n_ref[h, 0:1, :]
        qh = q[:, h * dk:(h + 1) * dk]
        kh = k[:, h * dk:(h + 1) * dk]
        vh = z_ref[:, nqk + h * dv:nqk + (h + 1) * dv]
        oh = z_ref[:, nqk + MIX_W + h * dv:nqk + MIX_W + (h + 1) * dv]
        a_log = g_col + m_st
        d_log = jnp.where(tril, g_col - g_row + i_row, -jnp.inf)
        m_q = jnp.maximum(a_log, jnp.max(d_log, axis=-1, keepdims=True))
        inter = jnp.exp(a_log - m_q)
        s_w = lax.dot_general(qh.astype(BF16), kh.astype(BF16), (((1,), (1,)), ((), ())),
                              preferred_element_type=F32) * jnp.exp(d_log - m_q)
        num = inter * _dot(qh, c_st) + _dot(s_w, vh)
        den = inter * jnp.sum(qh * n_st, axis=-1, keepdims=True) + jnp.sum(s_w, axis=-1, keepdims=True)
        hh = num / jnp.maximum(jnp.abs(den), jnp.exp(-m_q))
        g_tot = g_col[ln - 1:ln, :]
        w_log = g_tot - g_col + i_col
        m_new = jnp.maximum(g_tot + m_st, jnp.max(w_log, axis=0, keepdims=True))
        decay = jnp.exp(g_tot + m_st - m_new)
        kw = kh * jnp.exp(w_log - m_new)
        c_ref[h] = decay * c_st + lax.dot_general(kw.astype(BF16), vh.astype(BF16), (((0,), (0,)), ((), ())),
                                                  preferred_element_type=F32)
        n_ref[h] = jnp.broadcast_to(decay * n_st + jnp.sum(kw, axis=0, keepdims=True), (8, dk))
        m_ref[h] = jnp.broadcast_to(m_new, (8, 128))
        hn = hh * lax.rsqrt(jnp.mean(hh * hh, axis=-1, keepdims=True) + EPS) * hng_ref[:, h * dv:(h + 1) * dv]
        y_ref[:, h * dv:(h + 1) * dv] = (_sigmoid(oh) * hn).astype(y_ref.dtype)


def _mlstm_call(z_ml, zt, gate_row0, conv_w, b_col, b_row, hn_g):
    t = z_ml.shape[0]
    ln = ML_CHUNK
    tri = jnp.asarray(np.tril(np.ones((ln, ln))), BF16)
    full = lambda a: pl.BlockSpec(a.shape, lambda i: (0,) * a.ndim)
    return pl.pallas_call(
        _mlstm_kernel,
        grid=(t // ln,),
        in_specs=[pl.BlockSpec((ln, ML_COLS_PAD), lambda i: (i, 0)),
                  pl.BlockSpec((8, ln), lambda i: (gate_row0 // 8, i)),
                  full(conv_w), full(b_col), full(b_row), full(hn_g), full(tri)],
        out_specs=pl.BlockSpec((ln, MIX_W), lambda i: (i, 0)),
        out_shape=jax.ShapeDtypeStruct((t, MIX_W), BF16),
        scratch_shapes=[pltpu.VMEM((ML_HEADS, ML_QK, ML_V), F32),
                        pltpu.VMEM((ML_HEADS, 8, ML_QK), F32),
                        pltpu.VMEM((ML_HEADS, 8, 128), F32),
                        pltpu.VMEM((8, 2 * ML_HEADS * ML_QK), F32)],
        compiler_params=_cparams("arbitrary"),
        name="mlstm",
    )(z_ml, zt, conv_w, b_col, b_row, hn_g, tri)


DSA_TQ = 256
DSA_GROUP = 4
INT_MIN = -2 ** 31
NEG = -1e30
LOG2E = math.log2(math.e)
F32_MIN_NORMAL = 2.0 ** -126
SHIFT_SLACK = 8.0


def _t5_bucket_bounds():
    max_exact = REL_BUCKETS // 2
    d = np.arange(1, 4 * REL_MAX_DIST, dtype=np.float64)
    large = max_exact + np.floor(np.log(d / max_exact) / math.log(REL_MAX_DIST / max_exact)
                                 * (REL_BUCKETS - max_exact)).astype(np.int64)
    bucket = np.where(d < max_exact, d.astype(np.int64), np.minimum(large, REL_BUCKETS - 1))
    return [int(d[np.argmax(bucket >= b)]) for b in range(max_exact + 1, REL_BUCKETS)]


def _bias_tile_kernel(rb_ref, o_ref):
    o = pl.program_id(0)
    h = pl.program_id(1)
    tq = o_ref.shape[-1]
    s_loc = lax.broadcasted_iota(jnp.int32, (tq, tq), 0)
    t_loc = lax.broadcasted_iota(jnp.int32, (tq, tq), 1)
    dist = o * tq + t_loc - s_loc
    max_exact = REL_BUCKETS // 2
    bucket = jnp.full((tq, tq), max_exact, jnp.int32)
    for bnd in _t5_bucket_bounds():
        bucket = bucket + (dist >= bnd).astype(jnp.int32)
    bucket = jnp.where(dist < max_exact, jnp.maximum(dist, 0), bucket)
    bias = jnp.zeros((tq, tq), F32)
    for b in range(REL_BUCKETS):
        bias = jnp.where(bucket == b, rb_ref[b, h], bias)
    o_ref[...] = (bias - rb_ref[REL_BUCKETS - 1, h]) * LOG2E


def _bias_tiles(rel_bias, tq):
    assert tq >= REL_MAX_DIST
    return pl.pallas_call(
        _bias_tile_kernel,
        grid=(3, DSA_HEADS),
        in_specs=[pl.BlockSpec(memory_space=pltpu.SMEM)],
        out_specs=pl.BlockSpec((None, None, tq, tq), lambda o, h: (o, h, 0, 0)),
        out_shape=jax.ShapeDtypeStruct((3, DSA_HEADS, tq, tq), F32),
        name="t5_bias_tiles",
    )(rel_bias)


def _dsa_prep_kernel(zt_ref, ckvt_ref, g_ref, gcol_ref, gk_ref, ck_ref, ki_ref, ct_ref):
    ckv = zt_ref[:, 0:DSA_KV_RANK]
    ck_ref[...] = _rms(ckv, g_ref[...]).astype(BF16)
    ki_ref[...] = _rms(zt_ref[:, DSA_KV_RANK:DSA_KV_RANK + IDX_DIM], gk_ref[...]).astype(BF16)
    ct = ckvt_ref[...]
    ms = jnp.mean(ct * ct, axis=0, keepdims=True)
    ct_ref[...] = (ct * lax.rsqrt(ms + EPS) * gcol_ref[...]).astype(BF16)


def _dsa_prep(z_dtok, zt, ckv_row0, g_row, g_col, gk_row):
    t = z_dtok.shape[0]
    tq = DSA_TQ
    full = lambda a: pl.BlockSpec(a.shape, lambda i: (0,) * a.ndim)
    return pl.pallas_call(
        _dsa_prep_kernel,
        grid=(t // tq,),
        in_specs=[pl.BlockSpec((tq, z_dtok.shape[1]), lambda i: (i, 0)),
                  pl.BlockSpec((DSA_KV_RANK, tq), lambda i: (ckv_row0 // DSA_KV_RANK, i)),
                  full(g_row), full(g_col), full(gk_row)],
        out_specs=[pl.BlockSpec((tq, DSA_KV_RANK), lambda i: (i, 0)),
                   pl.BlockSpec((tq, IDX_DIM), lambda i: (i, 0)),
                   pl.BlockSpec((None, DSA_KV_RANK, tq), lambda i: (i, 0, 0))],
        out_shape=[jax.ShapeDtypeStruct((t, DSA_KV_RANK), BF16),
                   jax.ShapeDtypeStruct((t, IDX_DIM), BF16),
                   jax.ShapeDtypeStruct((t // tq, DSA_KV_RANK, tq), BF16)],
        compiler_params=_cparams("parallel"),
        name="dsa_prep",
    )(z_dtok, zt, g_row, g_col, gk_row)


def _dsa_kernel(topk, qt_ref, qit_ref, wt_ref, ki_ref, ck_ref, ct_ref, wuk_ref, wuv_ref, bias_ref,
                o_ref, key_ref, acc_ref, m_ref, l_ref, qlat_ref, qi_ref, s_ref, tmax_ref, hi_ref):
    i = pl.program_id(0)
    tq = DSA_TQ
    nkb = i + 1
    nh, hd, r = DSA_HEADS, DSA_HEAD_DIM, DSA_KV_RANK

    for h in range(nh):
        ql = jnp.dot(wuk_ref[h], qt_ref[h * hd:(h + 1) * hd, :].astype(BF16), preferred_element_type=F32)
        qlat_ref[h] = (ql * (hd ** -0.5 * LOG2E)).astype(BF16)
    qi_ref[...] = qit_ref[...].astype(BF16)
    wrow = wt_ref[...] * ((IDX_DIM ** -0.5) * (IDX_HEADS ** -0.5))
    s_loc = lax.broadcasted_iota(jnp.int32, (tq, tq), 0)
    t_loc = lax.broadcasted_iota(jnp.int32, (tq, tq), 1)

    def score_blk(kb, carry):
        kblk = ki_ref[pl.ds(pl.multiple_of(kb * tq, tq), tq), :]
        acc = jnp.zeros((tq, tq), F32)
        for h in range(IDX_HEADS):
            rel = jnp.dot(kblk, qi_ref[h * IDX_DIM:(h + 1) * IDX_DIM, :], preferred_element_type=F32)
            acc = acc + wrow[h:h + 1, :] * jnp.maximum(rel, 0.0)
        acc = jnp.where(jnp.abs(acc) < F32_MIN_NORMAL, 0.0, acc)
        bits = pltpu.bitcast(acc, jnp.int32)
        key = jnp.where(bits < 0, bits ^ 0x7FFFFFFF, bits)
        causal = (kb * tq + s_loc) <= (i * tq + t_loc)
        key_ref[kb] = jnp.where(causal, key, INT_MIN)
        hi = pltpu.bitcast(bits & jnp.int32(-65536), F32)
        hi_ref[kb] = jnp.where(causal, hi, -jnp.inf).astype(BF16)
        return carry

    def score_pair(g, carry):
        score_blk(2 * g, carry)
        return score_blk(2 * g + 1, carry)

    lax.fori_loop(0, nkb // 2, score_pair, 0)

    @pl.when(nkb % 2 == 1)
    def _():
        score_blk(nkb - 1, 0)

    grp = DSA_GROUP
    ngrp = (i + grp) // grp

    def pad_blk(kb, carry):
        key_ref[kb] = jnp.full((tq, tq), INT_MIN, jnp.int32)
        hi_ref[kb] = jnp.full((tq, tq), -jnp.inf, BF16)
        return carry

    lax.fori_loop(nkb, ngrp * grp, pad_blk, 0)

    def count_ge(cand):
        def body(g, cnt):
            for j in range(grp):
                ge = (key_ref[g * grp + j] >= cand).astype(jnp.int32)
                cnt = cnt + jnp.sum(ge.reshape(tq // 8, 8, tq), axis=0)
            return cnt
        cnt = lax.fori_loop(0, ngrp, body, jnp.zeros((8, tq), jnp.int32))
        return jnp.sum(cnt, axis=0, keepdims=True)

    few = (i * tq + t_loc[0:1, :]) < topk

    pk = tq // 16

    def count_hi(c_row):
        ct = jnp.broadcast_to(c_row.astype(BF16), (16, tq))
        one, nil = jnp.ones((), BF16), jnp.zeros((), BF16)

        def body(g, cnt):
            for j in range(grp):
                hb = hi_ref[g * grp + j].reshape(pk, 16, tq)
                part = jnp.where(hb[0] >= ct, one, nil)
                for r in range(1, pk):
                    part = part + jnp.where(hb[r] >= ct, one, nil)
                cnt = cnt + part.astype(F32)
            return cnt
        cnt = lax.fori_loop(0, ngrp, body, jnp.zeros((16, tq), F32))
        return jnp.sum(cnt, axis=0, keepdims=True)

    def hi_body(step, cur16):
        cand16 = cur16 + jnp.left_shift(jnp.int32(1), 15 - step)
        pat = jnp.where(cand16 >= 0, cand16, cand16 ^ 0x7FFF) & 0xFFFF
        c = pltpu.bitcast(jnp.left_shift(pat, 16), F32)
        c = jnp.where(jnp.abs(c) < F32_MIN_NORMAL, jnp.where(cand16 > 0, F32_MIN_NORMAL, 0.0), c)
        return jnp.where(count_hi(c) >= topk, cand16, cur16)

    cur16 = lax.fori_loop(0, 16, hi_body, jnp.full((1, tq), -32768, jnp.int32))

    def search_cond(c):
        return jnp.logical_and(c[0] < 32, c[3] > 0)

    def search_body(c):
        step, cur, cnt, _ = c
        cand = cur + jnp.left_shift(jnp.int32(1), 31 - step)
        cnt_c = count_ge(cand)
        take = cnt_c >= topk
        cur = jnp.where(take, cand, cur)
        cnt = jnp.where(take, cnt_c, cnt)
        pending = jnp.sum(jnp.where(few, 0, (cnt != topk).astype(jnp.int32)))
        return step + 1, cur, cnt, pending

    init = (jnp.int32(16), cur16 * 65536, jnp.full((1, tq), -1, jnp.int32), jnp.int32(1))
    thr = lax.while_loop(search_cond, search_body, init)[1]
    thr = jnp.where(few, INT_MIN + 1, jnp.maximum(thr, INT_MIN + 1))

    m_ref[...] = jnp.full_like(m_ref, NEG)
    l_ref[...] = jnp.zeros_like(l_ref)
    acc_ref[...] = jnp.zeros_like(acc_ref)

    def att_grp(near, g, carry):
        excess = jnp.full((1, tq), NEG, F32)
        tmax = [None] * nh
        for j in range(grp):
            kb = g * grp + j
            addm = jnp.where(key_ref[kb] >= thr, 0.0, NEG)
            cblk = ck_ref[pl.ds(pl.multiple_of(kb * tq, tq), tq), :]
            for h in range(nh):
                s = jnp.dot(cblk, qlat_ref[h], preferred_element_type=F32) + addm
                if near:
                    s = s + bias_ref[jnp.clip(i - kb, 0, 2), h]
                s_ref[j * nh + h] = s
                tm = jnp.max(s, axis=0, keepdims=True)
                tmax[h] = tm if j == 0 else jnp.maximum(tmax[h], tm)
        for h in range(nh):
            tmax_ref[h] = tmax[h]
            excess = jnp.maximum(excess, tmax[h] - m_ref[h])

        @pl.when(jnp.max(excess) > SHIFT_SLACK)
        def _():
            for h in range(nh):
                m_old = m_ref[h]
                m_new = jnp.maximum(m_old, tmax_ref[h])
                corr = jnp.exp2(m_old - m_new)
                l_ref[h] = l_ref[h] * corr
                acc_ref[h] = acc_ref[h] * corr
                m_ref[h] = m_new

        ctg = jnp.concatenate([ct_ref[g * grp + j] for j in range(grp)], axis=1)
        for h in range(nh):
            m_h = m_ref[h]
            p = jnp.concatenate([jnp.exp2(s_ref[j * nh + h] - m_h) for j in range(grp)], axis=0)
            l_ref[h] = l_ref[h] + jnp.sum(p, axis=0, keepdims=True)
            acc_ref[h] = acc_ref[h] + jnp.dot(ctg, p.astype(BF16), preferred_element_type=F32)
        return carry

    nfar = jnp.maximum(i - 1, 0) // grp
    lax.fori_loop(0, nfar, functools.partial(att_grp, False), 0)
    lax.fori_loop(nfar, ngrp, functools.partial(att_grp, True), 0)

    for h in range(nh):
        o_lat = (acc_ref[h] / l_ref[h]).astype(BF16)
        out_t = jnp.dot(wuv_ref[h], o_lat, preferred_element_type=F32)
        o_ref[:, h * hd:(h + 1) * hd] = out_t.T.astype(o_ref.dtype)


def _dsa_call(zt, q_row0, qi_row0, w_row0, ki, ck, ct, wuk_t, wuv_t, bias_tiles, topk):
    t = zt.shape[1]
    tq = DSA_TQ
    assert (t // tq) % DSA_GROUP == 0, "key tiles are walked in whole groups"
    nq = DSA_HEADS * DSA_HEAD_DIM
    ni = IDX_HEADS * IDX_DIM
    full = lambda a: pl.BlockSpec(a.shape, lambda i: (0,) * a.ndim, pipeline_mode=pl.Buffered(1))
    return pl.pallas_call(
        functools.partial(_dsa_kernel, topk),
        grid=(t // tq,),
        in_specs=[pl.BlockSpec((nq, tq), lambda i: (q_row0 // nq, i)),
                  pl.BlockSpec((ni, tq), lambda i: (qi_row0 // ni, i)),
                  pl.BlockSpec((IDX_HEADS, tq), lambda i: (w_row0 // IDX_HEADS, i)),
                  full(ki), full(ck), full(ct), full(wuk_t), full(wuv_t), full(bias_tiles)],
        out_specs=pl.BlockSpec((tq, MIX_W), lambda i: (i, 0)),
        out_shape=jax.ShapeDtypeStruct((t, MIX_W), BF16),
        scratch_shapes=[pltpu.VMEM((pl.cdiv(t // tq, DSA_GROUP) * DSA_GROUP, tq, tq), jnp.int32),
                        pltpu.VMEM((DSA_HEADS, DSA_KV_RANK, tq), F32),
                        pltpu.VMEM((DSA_HEADS, 1, tq), F32),
                        pltpu.VMEM((DSA_HEADS, 1, tq), F32),
                        pltpu.VMEM((DSA_HEADS, DSA_KV_RANK, tq), BF16),
                        pltpu.VMEM((ni, tq), BF16),
                        pltpu.VMEM((DSA_GROUP * DSA_HEADS, tq, tq), F32),
                        pltpu.VMEM((DSA_HEADS, 1, tq), F32),
                        pltpu.VMEM((pl.cdiv(t // tq, DSA_GROUP) * DSA_GROUP, tq, tq), BF16)],
        compiler_params=_cparams("arbitrary"),
        name="dsa",
    )(zt, zt, zt, ki, ck, ct, wuk_t, wuv_t, bias_tiles)


FEAT_Q0 = 0
FEAT_QI0 = FEAT_Q0 + DSA_HEADS * DSA_HEAD_DIM
FEAT_CKV0 = FEAT_QI0 + IDX_HEADS * IDX_DIM
FEAT_W0 = FEAT_CKV0 + DSA_KV_RANK
FEAT_MLG0 = FEAT_W0 + IDX_HEADS
FEAT_ROWS = 2560
DTOK_COLS = 384


def _pad_last(a, n):
    return jnp.pad(a, [(0, 0)] * (a.ndim - 1) + [(0, n - a.shape[-1])])


def kernel(x, p, norm_gains, ffn1_gate, ffn1_up, ffn1_down, w_in, rw_mu, rw_w0, rw_w_up, rw_a0, rw_a_up, rw_g_up, rw_k_k, rw_k_a, rw_r_k, rw_lnx_g, rw_lnx_b, dsa_ckv_g, dsa_kidx_g, dsa_w_uk, dsa_w_uv, rel_bias, ml_conv, ml_b_i, ml_b_f, ml_hn_g, w_branch, w_out, ffn2_gate, ffn2_up, ffn2_down, ple_proj, ple_gate):
    b, t, d = x.shape
    assert b == 1 and d == D_MODEL
    nl = w_in.shape[0]
    h = x.reshape(t, d)
    p = p.reshape(nl, t, PLE_DIM)
    bf = lambda a: a.astype(BF16)

    f1g, f1u, f1d = bf(ffn1_gate), bf(ffn1_up), bf(ffn1_down)
    f2g, f2u, f2d = bf(ffn2_gate), bf(ffn2_up), bf(ffn2_down)
    wb, wo, wpg, wpp = bf(w_branch), bf(w_out), bf(ple_gate), bf(ple_proj)
    o_dsa = RW_COLS
    o_ml = o_dsa + DSA_COLS
    o_gate = o_ml + ML_COLS
    w_rw = bf(_pad_last(w_in[:, :, 0:RW_COLS], RW_COLS_PAD))
    w_ml = bf(_pad_last(w_in[:, :, o_ml:o_gate], ML_COLS_PAD))
    w_gate = bf(w_in[:, :, o_gate:])
    nq = DSA_HEADS * DSA_HEAD_DIM
    ni = IDX_HEADS * IDX_DIM
    w_q = w_in[:, :, o_dsa:o_dsa + nq]
    w_ckv = w_in[:, :, o_dsa + nq:o_dsa + nq + DSA_KV_RANK]
    w_qi = w_in[:, :, o_dsa + nq + DSA_KV_RANK:o_dsa + nq + DSA_KV_RANK + ni]
    w_ki = w_in[:, :, o_dsa + nq + DSA_KV_RANK + ni:o_dsa + nq + DSA_KV_RANK + ni + IDX_DIM]
    w_wi = w_in[:, :, o_dsa + DSA_COLS - IDX_HEADS:o_dsa + DSA_COLS]
    w_mlg = w_in[:, :, o_gate - 2 * ML_HEADS:o_gate]
    w_dtok = bf(_pad_last(jnp.concatenate([w_ckv, w_ki], axis=-1), DTOK_COLS))
    w_feat = bf(jnp.swapaxes(_pad_last(jnp.concatenate([w_q, w_qi, w_ckv, w_wi, w_mlg], axis=-1), FEAT_ROWS), 1, 2))
    mu = _pad_last(rw_mu, RW_COLS_PAD)
    wup, aup = bf(rw_w_up), bf(rw_a_up)
    gup = bf(jnp.pad(rw_g_up, ((0, 0), (0, RW_GATE_PAD - RW_GATE_LORA), (0, 0))))
    wuk_t = bf(jnp.swapaxes(dsa_w_uk, 2, 3))
    wuv_t = bf(jnp.swapaxes(dsa_w_uv, 2, 3))
    ml_b = jnp.concatenate([ml_b_i, ml_b_f], axis=-1)

    bias_tiles = _bias_tiles(rel_bias, DSA_TQ)
    topk = min(TOPK_MAX, t // 4)
    row = lambda a: a.reshape(1, -1)

    for i in range(nl):
        g = lambda k: norm_gains[i, k].reshape(1, d)
        h = _ffn(h, g(0), g(1), f1g, f1u, f1d, i)

        z_rw = _proj_tok(h, g(2), w_rw, i, tn=1152)
        z_ml = _proj_tok(h, g(2), w_ml, i, tn=640)
        z_gate = _proj_tok(h, g(2), w_gate, i, tn=1024, out_dtype=BF16)
        z_dtok = _proj_tok(h, g(2), w_dtok, i, tn=DTOK_COLS)
        zt = _proj_feat(h, g(2), w_feat, i, tn=1280)

        y_rw = _rwkv_call(z_rw, row(mu[i]), row(rw_w0[i]), wup[i], row(rw_a0[i]), aup[i], gup[i],
                          row(rw_k_k[i]), row(rw_k_a[i]), row(rw_r_k[i]), row(rw_lnx_g[i]), row(rw_lnx_b[i]))
        ck, ki, ct = _dsa_prep(z_dtok, zt, FEAT_CKV0, row(dsa_ckv_g[i]), dsa_ckv_g[i].reshape(-1, 1),
                               row(dsa_kidx_g[i]))
        y_dsa = _dsa_call(zt, FEAT_Q0, FEAT_QI0, FEAT_W0, ki, ck, ct, wuk_t[i], wuv_t[i], bias_tiles, topk)
        y_ml = _mlstm_call(z_ml, zt, FEAT_MLG0, ml_conv[i], row(ml_b[i]), ml_b[i].reshape(-1, 1), row(ml_hn_g[i]))

        h = _merge_out(h, y_rw, y_dsa, y_ml, z_gate, g(3), wb, wo, i)
        h = _ffn(h, g(4), g(5), f2g, f2u, f2d, i)
        h = _ple(h, p, g(6), g(7), wpg, wpp, i)
    return h.reshape(b, t, d)
```
